```python
import functools
import jax, jax.numpy as jnp
from jax import lax
import numpy as np

D_MODEL = 1024
BATCH = 8
SEQ = 2048
DEPTH = 2
DEC_BATCH = 32
DEC_SEQ = 4
PAST_LEN = 16384
PAGE_SIZE = 128

MIX_W = D_MODEL
FOX_W = MIX_W // 2
FOX_HEAD_DIM = 64
N_FOX_HEADS = FOX_W // FOX_HEAD_DIM
HGRN_W = MIX_W - FOX_W
HGRN_HEAD_DIM = 128
N_HGRN_HEADS = HGRN_W // HGRN_HEAD_DIM
D_FF = ((8 * D_MODEL // 3 + 127) // 128) * 128
CONV_W = 3
Q_BLOCK = 128
HGRN_CHUNK = 64
N_IN = 3 * FOX_W + N_FOX_HEADS + 4 * HGRN_W
SPLITS = (FOX_W, 2 * FOX_W, 3 * FOX_W, 3 * FOX_W + N_FOX_HEADS,
          3 * FOX_W + N_FOX_HEADS + HGRN_W, 3 * FOX_W + N_FOX_HEADS + 2 * HGRN_W,
          3 * FOX_W + N_FOX_HEADS + 3 * HGRN_W)
EPS = 1e-6
FOX_SCALE = FOX_HEAD_DIM ** -0.5
MASK_VALUE = -1e30
TINY = 1e-30

kernel_name = 'hymba_fox_hgrn2_convffn_step'


def rmsnorm(x, g):
    xf = x.astype(jnp.float32)
    y = xf * lax.rsqrt(jnp.mean(xf * xf, axis=-1, keepdims=True) + EPS)
    return (y * g.astype(jnp.float32)).astype(x.dtype)


def headnorm(o, g):
    of = o.astype(jnp.float32)
    y = of * lax.rsqrt(jnp.mean(of * of, axis=-1, keepdims=True) + EPS)
    return (y * g.astype(jnp.float32)).astype(o.dtype)


def fox_core(q, k, v, cq, ck, qpos, kpos):
    s = jnp.einsum('bqhd,bkhd->bhqk', q, k).astype(jnp.float32) * FOX_SCALE
    bias = jnp.transpose(cq, (0, 2, 1))[:, :, :, None] - jnp.transpose(ck, (0, 2, 1))[:, :, None, :]
    mask = qpos[:, None] >= kpos[None, :]
    p = jax.nn.softmax(jnp.where(mask[None, None], s + bias, MASK_VALUE), axis=-1)
    return jnp.einsum('bhqk,bkhd->bqhd', p.astype(v.dtype), v)


def fox_prompt(q, k, v, logf):
    B, T, H, Dh = q.shape
    C = jnp.cumsum(logf.astype(jnp.float32), axis=1)
    qb = min(Q_BLOCK, T)
    nb = T // qb
    kpos = jnp.arange(T)

    def blk(i):
        s0 = i * qb
        qs = lax.dynamic_slice_in_dim(q, s0, qb, axis=1)
        cq = lax.dynamic_slice_in_dim(C, s0, qb, axis=1)
        return fox_core(qs, k, v, cq, C, s0 + jnp.arange(qb), kpos)

    out = lax.map(blk, jnp.arange(nb))
    return jnp.moveaxis(out, 0, 1).reshape(B, T, H, Dh)


def fox_sample(q, k, v, logf, k_past, v_past, logf_past):
    T = q.shape[1]
    P = k_past.shape[1]
    k_all = jnp.concatenate([k_past.astype(k.dtype), k], axis=1)
    v_all = jnp.concatenate([v_past.astype(v.dtype), v], axis=1)
    C = jnp.cumsum(jnp.concatenate([logf_past.astype(jnp.float32), logf.astype(jnp.float32)], axis=1), axis=1)
    return fox_core(q, k_all, v_all, C[:, P:], C, P + jnp.arange(T), jnp.arange(P + T))


def hgrn2_chunked(q, logf, k, v, S0):
    B, T, H, DK = q.shape
    DV = v.shape[-1]
    L = HGRN_CHUNK if T % HGRN_CHUNK == 0 else T
    nc = T // L
    f32 = jnp.float32

    def to_chunks(a):
        return jnp.moveaxis(a.reshape((B, nc, L) + a.shape[2:]), 1, 0)

    causal = jnp.tril(jnp.ones((L, L), dtype=bool))[None, :, :, None, None]

    def step(S, inp):
        qc, lfc, kc, vc = inp
        qf, kf, vf = qc.astype(f32), kc.astype(f32), vc.astype(f32)
        b = jnp.cumsum(lfc.astype(f32), axis=1)
        inter = jnp.einsum('blhk,bhkv->blhv', qf * jnp.exp(b), S)
        diff = b[:, :, None] - b[:, None]
        decay = jnp.where(causal, jnp.exp(jnp.where(causal, diff, 0.0)), 0.0)
        A = jnp.einsum('bthk,btshk,bshk->bhts', qf, decay, kf)
        intra = jnp.einsum('bhts,bshv->bthv', A, vf)
        b_last = b[:, -1]
        S_new = jnp.exp(b_last)[..., None] * S + jnp.einsum('bshk,bshv->bhkv', kf * jnp.exp(b_last[:, None] - b), vf)
        return S_new, (inter + intra).astype(v.dtype)

    S_fin, o = lax.scan(step, S0.astype(f32), (to_chunks(q), to_chunks(logf), to_chunks(k), to_chunks(v)))
    return jnp.moveaxis(o, 0, 1).reshape(B, T, H, DV), S_fin


def layer(x, c, attend, S0, conv_buf, norm_g, w_ada, b_ada, w_in, fox_fbias, fox_onorm_g, lb,
          hgrn_onorm_g, w_out, w_up, conv_w, conv_b, w_down):
    B, T, _ = x.shape
    f32 = jnp.float32
    mod = (jax.nn.silu(c) @ w_ada + b_ada).reshape(B, 6, 1, D_MODEL)
    shift1, scale1, gate1, shift2, scale2, gate2 = (mod[:, j] for j in range(6))
    h = rmsnorm(x, norm_g[0]) * (1 + scale1) + shift1
    z = h @ w_in
    fq, fk, fv, ff, hq, hf, hi, hg = jnp.split(z, SPLITS, axis=-1)
    fs = (B, T, N_FOX_HEADS, FOX_HEAD_DIM)
    q, k, v = fq.reshape(fs), fk.reshape(fs), fv.reshape(fs)
    logf = jax.nn.log_sigmoid((ff + fox_fbias).astype(f32))
    fo = headnorm(attend(q, k, v, logf), fox_onorm_g.reshape(N_FOX_HEADS, FOX_HEAD_DIM))
    hs = (B, T, N_HGRN_HEADS, HGRN_HEAD_DIM)
    lb_h = lb.reshape(N_HGRN_HEADS, HGRN_HEAD_DIM).astype(f32)
    f_h = lb_h + (1.0 - lb_h) * jax.nn.sigmoid(hf.reshape(hs).astype(f32))
    logf_h = jnp.log(jnp.maximum(f_h, TINY))
    ho, S_new = hgrn2_chunked(hq.reshape(hs), logf_h, 1.0 - f_h, jax.nn.silu(hi.reshape(hs)), S0)
    ho = headnorm(ho, hgrn_onorm_g.reshape(N_HGRN_HEADS, HGRN_HEAD_DIM)) * jax.nn.silu(hg.reshape(hs))
    mixed = jnp.concatenate([fo.reshape(B, T, FOX_W), ho.reshape(B, T, HGRN_W)], axis=-1) @ w_out
    x = x + gate1 * mixed
    h2 = rmsnorm(x, norm_g[1]) * (1 + scale2) + shift2
    a, b = jnp.split(h2 @ w_up, 2, axis=-1)
    a_all = jnp.concatenate([conv_buf.astype(a.dtype), a], axis=1)
    a_conv = conv_b + sum(conv_w[j] * a_all[:, j:j + T] for j in range(CONV_W))
    x = x + gate2 * ((jax.nn.gelu(a_conv) * b) @ w_down)
    return x, k, v, logf.astype(x.dtype), S_new.astype(x.dtype), a_all[:, T:]


def setup_inputs(seed: int = 0) -> dict:
    key = jax.random.key(seed)
    ks = jax.random.split(key, 24)
    f32 = jnp.float32

    def nrm(k, shape, s):
        return s * jax.random.normal(k, shape, f32)

    n_pages = PAST_LEN // PAGE_SIZE
    n_used = DEC_BATCH * n_pages
    n_pool = n_used + max(1, n_used // 4)
    page_table = jax.random.permutation(ks[7], n_pool)[:n_used].reshape(DEC_BATCH, n_pages).astype(jnp.int32)
    return {
        'x_prompt': nrm(ks[0], (BATCH, SEQ, D_MODEL), 1.0),
        'x_sample': nrm(ks[1], (DEC_BATCH, DEC_SEQ, D_MODEL), 1.0),
        'c_prompt': nrm(ks[2], (BATCH, D_MODEL), 1.0),
        'c_sample': nrm(ks[3], (DEC_BATCH, D_MODEL), 1.0),
        'cache_k': nrm(ks[4], (DEPTH, n_pool, PAGE_SIZE, N_FOX_HEADS, FOX_HEAD_DIM), 1.0),
        'cache_v': nrm(ks[5], (DEPTH, n_pool, PAGE_SIZE, N_FOX_HEADS, FOX_HEAD_DIM), 1.0),
        'cache_logf': jax.nn.log_sigmoid(1.0 + nrm(ks[6], (DEPTH, n_pool, PAGE_SIZE, N_FOX_HEADS), 0.5)),
        'page_table': page_table,
        'state_hgrn': nrm(ks[8], (DEPTH, DEC_BATCH, N_HGRN_HEADS, HGRN_HEAD_DIM, HGRN_HEAD_DIM), 0.5),
        'state_conv': nrm(ks[9], (DEPTH, DEC_BATCH, CONV_W - 1, D_FF), 1.0),
        'norm_g': 1.0 + nrm(ks[10], (DEPTH, 2, D_MODEL), 0.1),
        'final_norm_g': 1.0 + nrm(ks[11], (D_MODEL,), 0.1),
        'w_ada': nrm(ks[12], (DEPTH, D_MODEL, 6 * D_MODEL), 0.5 * D_MODEL ** -0.5),
        'b_ada': nrm(ks[13], (DEPTH, 6 * D_MODEL), 0.02),
        'w_in': nrm(ks[14], (DEPTH, D_MODEL, N_IN), D_MODEL ** -0.5),
        'fox_fbias': 1.0 + nrm(ks[15], (DEPTH, N_FOX_HEADS), 0.5),
        'fox_onorm_g': 1.0 + nrm(ks[16], (DEPTH, FOX_W), 0.1),
        'hgrn_lb_logits': nrm(ks[17], (DEPTH, HGRN_W), 0.5),
        'hgrn_onorm_g': 1.0 + nrm(ks[18], (DEPTH, HGRN_W), 0.1),
        'w_out': nrm(ks[19], (DEPTH, MIX_W, D_MODEL), MIX_W ** -0.5),
        'w_up': nrm(ks[20], (DEPTH, D_MODEL, 2 * D_FF), D_MODEL ** -0.5),
        'conv_w': nrm(ks[21], (DEPTH, CONV_W, D_FF), CONV_W ** -0.5),
        'conv_b': nrm(ks[22], (DEPTH, D_FF), 0.02),
        'w_down': nrm(ks[23], (DEPTH, D_FF, D_MODEL), D_FF ** -0.5),
    }


def reference(x_prompt, x_sample, c_prompt, c_sample, cache_k, cache_v, cache_logf, page_table,
              state_hgrn, state_conv, norm_g, final_norm_g, w_ada, b_ada, w_in, fox_fbias, fox_onorm_g,
              hgrn_lb_logits, hgrn_onorm_g, w_out, w_up, conv_w, conv_b, w_down):
    Bp = x_prompt.shape[0]
    Bs = x_sample.shape[0]
    P = page_table.shape[1] * cache_k.shape[2]
    p = jax.nn.softmax(hgrn_lb_logits.astype(jnp.float32), axis=0)
    lb_all = jnp.maximum(jnp.cumsum(p, axis=0) - p[0:1], 0.0)
    xp, xs = x_prompt, x_sample
    kp_l, vp_l, lfp_l, sp_l, cp_l = [], [], [], [], []
    ks_l, vs_l, lfs_l, ss_l, cs_l = [], [], [], [], []
    for l in range(DEPTH):
        w = (norm_g[l], w_ada[l], b_ada[l], w_in[l], fox_fbias[l], fox_onorm_g[l], lb_all[l],
             hgrn_onorm_g[l], w_out[l], w_up[l], conv_w[l], conv_b[l], w_down[l])
        S0p = jnp.zeros((Bp, N_HGRN_HEADS, HGRN_HEAD_DIM, HGRN_HEAD_DIM), xp.dtype)
        buf0 = jnp.zeros((Bp, CONV_W - 1, D_FF), xp.dtype)
        xp, kp, vp, lfp, sp, cp = layer(xp, c_prompt, fox_prompt, S0p, buf0, *w)
        kp_l.append(kp); vp_l.append(vp); lfp_l.append(lfp); sp_l.append(sp); cp_l.append(cp)
        k_past = cache_k[l, page_table].reshape(Bs, P, N_FOX_HEADS, FOX_HEAD_DIM)
        v_past = cache_v[l, page_table].reshape(Bs, P, N_FOX_HEADS, FOX_HEAD_DIM)
        lf_past = cache_logf[l, page_table].reshape(Bs, P, N_FOX_HEADS)
        attend_s = functools.partial(fox_sample, k_past=k_past, v_past=v_past, logf_past=lf_past)
        xs, ksm, vsm, lfs, ss, cs = layer(xs, c_sample, attend_s, state_hgrn[l], state_conv[l], *w)
        ks_l.append(ksm); vs_l.append(vsm); lfs_l.append(lfs); ss_l.append(ss); cs_l.append(cs)
    y_prompt = rmsnorm(xp, final_norm_g)
    y_sample = rmsnorm(xs, final_norm_g)
    return (y_prompt, y_sample,
            jnp.stack(kp_l), jnp.stack(vp_l), jnp.stack(lfp_l), jnp.stack(sp_l), jnp.stack(cp_l),
            jnp.stack(ks_l), jnp.stack(vs_l), jnp.stack(lfs_l), jnp.stack(ss_l), jnp.stack(cs_l))
```

```python
import functools

import numpy as np
import jax
import jax.numpy as jnp
from jax import lax
from jax.experimental import pallas as pl
from jax.experimental.pallas import tpu as pltpu

F32 = jnp.float32
BF16 = jnp.bfloat16
HIGHEST = lax.Precision.HIGHEST

D_MODEL = 1024
FOX_W = 512
FOX_HEAD_DIM = 64
N_FOX_HEADS = 8
HGRN_W = 512
HGRN_HEAD_DIM = 128
N_HGRN_HEADS = 4
D_FF = 2816
EPS = 1e-6
FOX_SCALE = FOX_HEAD_DIM ** -0.5
MASK_VALUE = -1e30
TINY = 1e-30
NEG_INIT = -1e30

HGRN_CHUNK = 64
PAGES_PER_STEP = 8
BIAS_PAGES_PER_STEP = 16
VMEM_LIMIT = 56 * 1024 * 1024


def _cparams(*sem):
    return pltpu.CompilerParams(dimension_semantics=sem, vmem_limit_bytes=VMEM_LIMIT)


def _sigmoid(x):
    return 1.0 / (1.0 + jnp.exp(-x))


def _silu(x):
    return x * _sigmoid(x)


def _log_sigmoid(x):
    return jnp.minimum(x, 0.0) - jnp.log(1.0 + jnp.exp(-jnp.abs(x)))


def _gelu_tanh(x):
    c = np.float32(np.sqrt(2.0 / np.pi))
    return 0.5 * x * (1.0 + jnp.tanh(c * (x + 0.044715 * (x * x * x))))


def _rms(x):
    return x * lax.rsqrt(jnp.mean(x * x, axis=-1, keepdims=True) + EPS)


def _dot(a, b):
    return jnp.dot(a, b, preferred_element_type=F32)


def _dot_nt(a, b):
    return lax.dot_general(a, b, (((1,), (1,)), ((), ())), preferred_element_type=F32)


def _dot_tn(a, b):
    return lax.dot_general(a, b, (((0,), (0,)), ((), ())), preferred_element_type=F32)


def _ada_kernel(c_ref, w_ref, b_ref, o_ref):
    a = _silu(c_ref[...])
    o_ref[0] = jnp.dot(a, w_ref[0], precision=HIGHEST, preferred_element_type=F32) + b_ref[0]


def _ada(c_all, w_ada, b_ada):
    depth, d, n = w_ada.shape
    nc = c_all.shape[0]
    tn = 1536
    return pl.pallas_call(
        _ada_kernel,
        grid=(depth, n // tn),
        in_specs=[pl.BlockSpec((nc, d), lambda l, j: (0, 0)),
                  pl.BlockSpec((1, d, tn), lambda l, j: (l, 0, j)),
                  pl.BlockSpec((1, 1, tn), lambda l, j: (l, 0, j))],
        out_specs=pl.BlockSpec((1, nc, tn), lambda l, j: (l, 0, j)),
        out_shape=jax.ShapeDtypeStruct((depth, nc, n), F32),
        compiler_params=_cparams("arbitrary", "arbitrary"),
        name="ada",
    )(c_all, w_ada, b_ada.reshape(depth, 1, n))


def _inproj_kernel(x_ref, sc_ref, sh_ref, g_ref, wm_ref, wf_ref, wft_ref, fb_ref, fbt_ref,
                   q_ref, k_ref, v_ref, kb_ref, vb_ref, lf_ref, lft_ref,
                   hq_ref, hf_ref, hi_ref, hg_ref):
    h = _rms(x_ref[0]) * g_ref[...]
    h = h * (1.0 + sc_ref[0]) + sh_ref[0]
    hb = h.astype(BF16)

    def grp(i):
        return _dot(hb, wm_ref[:, i * 512:(i + 1) * 512])

    q_ref[0] = (grp(0) * FOX_SCALE).astype(BF16)
    zk = grp(1)
    k_ref[0] = zk
    kb_ref[0] = zk.astype(BF16)
    zv = grp(2)
    v_ref[0] = zv
    vb_ref[0] = zv.astype(BF16)
    hq_ref[0] = grp(3)
    hf_ref[0] = grp(4)
    hi_ref[0] = grp(5)
    hg_ref[0] = grp(6)
    zf = _dot(hb, wf_ref[...])
    lf_ref[0] = _log_sigmoid(zf[:, :N_FOX_HEADS] + fb_ref[...])
    lft_ref[0] = _log_sigmoid(_dot_nt(wft_ref[...], hb) + fbt_ref[...])


def _inproj(x, sc, sh, g, wm, wf, wft, fb, fbt, tm):
    bx, tx, d = x.shape
    tmod = sc.shape[1]
    per_row = tmod != 1
    mod_spec = pl.BlockSpec((1, tm if per_row else 1, d),
                            (lambda b, i: (b, i, 0)) if per_row else (lambda b, i: (b, 0, 0)))
    row = lambda c: pl.BlockSpec((1, tm, c), lambda b, i: (b, i, 0))
    const = lambda a: pl.BlockSpec(a.shape, lambda b, i: (0,) * a.ndim)
    sds = jax.ShapeDtypeStruct
    outs = [sds((bx, tx, 512), BF16), sds((bx, tx, 512), F32), sds((bx, tx, 512), F32),
            sds((bx, tx, 512), BF16), sds((bx, tx, 512), BF16),
            sds((bx, tx, N_FOX_HEADS), F32), sds((bx, N_FOX_HEADS, tx), F32),
            sds((bx, tx, 512), F32), sds((bx, tx, 512), F32), sds((bx, tx, 512), F32),
            sds((bx, tx, 512), F32)]
    out_specs = [row(512)] * 5 + [row(N_FOX_HEADS),
                                  pl.BlockSpec((1, N_FOX_HEADS, tm), lambda b, i: (b, 0, i))] + [row(512)] * 4
    return pl.pallas_call(
        _inproj_kernel,
        grid=(bx, tx // tm),
        in_specs=[row(d), mod_spec, mod_spec, const(g), const(wm), const(wf), const(wft),
                  const(fb), const(fbt)],
        out_specs=out_specs,
        out_shape=outs,
        compiler_params=_cparams("arbitrary", "arbitrary"),
        name="inproj",
    )(x, sc, sh, g, wm, wf, wft, fb, fbt)


def _cumsum_kernel(lf_ref, lft_ref, tril_ref, triu_ref, ccol_ref, crow_ref, *, blk):
    t = lf_ref.shape[1]
    nh = lf_ref.shape[2]
    crow_carry = jnp.zeros((nh, 1), F32)
    ccol_carry = jnp.zeros((1, nh), F32)
    for i in range(t // blk):
        sl = slice(i * blk, (i + 1) * blk)
        r = jnp.dot(lft_ref[0, :, sl], triu_ref[...], precision=HIGHEST,
                    preferred_element_type=F32) + crow_carry
        crow_ref[0, :, sl] = r
        crow_carry = r[:, blk - 1:blk]
        c = jnp.dot(tril_ref[...], lf_ref[0, sl, :], precision=HIGHEST,
                    preferred_element_type=F32) + ccol_carry
        for h in range(nh):
            ccol_ref[0, h, sl, :] = c[:, h:h + 1]
        ccol_carry = c[blk - 1:blk, :]


def _cumsum(lf, lft):
    b, t, nh = lf.shape
    blk = 256
    tril = jnp.asarray(np.tril(np.ones((blk, blk), np.float32)))
    triu = jnp.asarray(np.triu(np.ones((blk, blk), np.float32)))
    return pl.pallas_call(
        functools.partial(_cumsum_kernel, blk=blk),
        grid=(b,),
        in_specs=[pl.BlockSpec((1, t, nh), lambda i: (i, 0, 0)),
                  pl.BlockSpec((1, nh, t), lambda i: (i, 0, 0)),
                  pl.BlockSpec((blk, blk), lambda i: (0, 0)),
                  pl.BlockSpec((blk, blk), lambda i: (0, 0))],
        out_specs=[pl.BlockSpec((1, nh, t, 1), lambda i: (i, 0, 0, 0)),
                   pl.BlockSpec((1, nh, t), lambda i: (i, 0, 0))],
        out_shape=[jax.ShapeDtypeStruct((b, nh, t, 1), F32),
                   jax.ShapeDtypeStruct((b, nh, t), F32)],
        compiler_params=_cparams("arbitrary"),
        name="cumsum",
    )(lf, lft, tril, triu)


def _fox_kernel(q_ref, k_ref, v_ref, ccol_ref, crow_ref, g_ref, o_ref, m_sc, l_sc, acc_sc, *, tq):
    qi = pl.program_id(2)
    q = q_ref[0]
    lane = lax.broadcasted_iota(jnp.int32, (tq, 128), 1)
    lo = lane < FOX_HEAD_DIM
    rowi = lax.broadcasted_iota(jnp.int32, (tq, tq), 0)
    coli = lax.broadcasted_iota(jnp.int32, (tq, tq), 1)
    causal = rowi >= coli
    outs = []
    for e in range(2):
        qe = jnp.where(lo if e == 0 else jnp.logical_not(lo), q, jnp.zeros_like(q))
        cq = ccol_ref[0, 0, e]
        m_sc[...] = jnp.full(m_sc.shape, NEG_INIT, F32)
        l_sc[...] = jnp.zeros(l_sc.shape, F32)
        acc_sc[...] = jnp.zeros(acc_sc.shape, F32)

        def step(kj, masked):
            start = pl.multiple_of(kj * tq, tq)
            kb = k_ref[0, pl.ds(start, tq), :]
            vb = v_ref[0, pl.ds(start, tq), :]
            ck = crow_ref[0, 0, pl.ds(e, 1), pl.ds(start, tq)]
            s = _dot_nt(qe, kb) + (cq - ck)
            if masked:
                s = jnp.where(causal, s, MASK_VALUE)
            m_prev = m_sc[...]
            m_new = jnp.maximum(m_prev, jnp.max(s, axis=-1, keepdims=True))
            alpha = jnp.exp(m_prev - m_new)
            p = jnp.exp(s - m_new)
            l_sc[...] = alpha * l_sc[...] + jnp.sum(p, axis=-1, keepdims=True)
            acc_sc[...] = alpha * acc_sc[...] + _dot(p.astype(BF16), vb)
            m_sc[...] = m_new

        def body(kj, carry):
            step(kj, False)
            return carry

        lax.fori_loop(0, qi, body, 0)
        step(qi, True)
        outs.append(acc_sc[...] / l_sc[...])
    o = jnp.where(lo, outs[0], outs[1])
    sq = o * o
    s_lo = jnp.sum(jnp.where(lo, sq, 0.0), axis=-1, keepdims=True)
    s_all = jnp.sum(sq, axis=-1, keepdims=True)
    ms = jnp.where(lo, s_lo, s_all - s_lo) * (1.0 / FOX_HEAD_DIM)
    o_ref[0] = (o * lax.rsqrt(ms + EPS) * g_ref[...]).astype(o_ref.dtype)


def _fox_prompt(q, kb, vb, ccol, crow, g):
    b, t, _ = q.shape
    tq = 256
    nhp = N_FOX_HEADS // 2
    ccol5 = ccol.reshape(b, nhp, 2, t, 1)
    crow4 = crow.reshape(b, nhp, 2, t)
    return pl.pallas_call(
        functools.partial(_fox_kernel, tq=tq),
        grid=(b, nhp, t // tq),
        in_specs=[pl.BlockSpec((1, tq, 128), lambda i, h, j: (i, j, h)),
                  pl.BlockSpec((1, t, 128), lambda i, h, j: (i, 0, h)),
                  pl.BlockSpec((1, t, 128), lambda i, h, j: (i, 0, h)),
                  pl.BlockSpec((1, 1, 2, tq, 1), lambda i, h, j: (i, h, 0, j, 0)),
                  pl.BlockSpec((1, 1, 2, t), lambda i, h, j: (i, h, 0, 0)),
                  pl.BlockSpec((1, 128), lambda i, h, j: (0, h))],
        out_specs=pl.BlockSpec((1, tq, 128), lambda i, h, j: (i, j, h)),
        out_shape=jax.ShapeDtypeStruct((b, t, FOX_W), BF16),
        scratch_shapes=[pltpu.VMEM((tq, 1), F32), pltpu.VMEM((tq, 1), F32),
                        pltpu.VMEM((tq, 128), F32)],
        compiler_params=_cparams("arbitrary", "arbitrary", "arbitrary"),
        name="fox",
    )(q, kb, vb, ccol5, crow4, g)


def _pbias_kernel(pt_ref, u_ref, *refs, gp):
    lf_refs = refs[:gp]
    o_ref = refs[gp]
    carry_sc = refs[gp + 1]

    @pl.when(pl.program_id(1) == 0)
    def _():
        carry_sc[...] = jnp.zeros(carry_sc.shape, F32)

    carry = carry_sc[...]
    for j in range(gp - 1, -1, -1):
        r = jnp.dot(lf_refs[j][0, 0], u_ref[...], precision=HIGHEST, preferred_element_type=F32)
        o_ref[0, j] = r[:, :128] + carry
        carry = carry + r[:, 128:]
    carry_sc[...] = carry


def _pbias(page_table, lft_cache, layer):
    bs, n_pages = page_table.shape
    gp = BIAS_PAGES_PER_STEP
    ng = n_pages // gp
    u = np.zeros((128, 256), np.float32)
    u[:, :128] = np.tril(np.ones((128, 128), np.float32), -1)
    u[:, 128:] = 1.0
    in_specs = [pl.BlockSpec((128, 256), lambda b, g, pt: (0, 0))]
    for j in range(gp):
        in_specs.append(pl.BlockSpec(
            (1, 1, N_FOX_HEADS, 128),
            lambda b, g, pt, j=j: (layer, pt[b, (ng - 1 - g) * gp + j], 0, 0)))
    grid_spec = pltpu.PrefetchScalarGridSpec(
        num_scalar_prefetch=1, grid=(bs, ng), in_specs=in_specs,
        out_specs=pl.BlockSpec((1, gp, N_FOX_HEADS, 128), lambda b, g, pt: (b, ng - 1 - g, 0, 0)),
        scratch_shapes=[pltpu.VMEM((N_FOX_HEADS, 128), F32)])
    return pl.pallas_call(
        functools.partial(_pbias_kernel, gp=gp),
        grid_spec=grid_spec,
        out_shape=jax.ShapeDtypeStruct((bs, n_pages, N_FOX_HEADS, 128), F32),
        compiler_params=_cparams("arbitrary", "arbitrary"),
        name="pbias",
    )(page_table, jnp.asarray(u), *([lft_cache] * gp))


def _decode_kernel(pt_ref, qbd_ref, cn_ref, bias_ref, knew_ref, vnew_ref, bnew_ref, g_ref, *refs,
                   gp, ng, nq):
    k_refs = refs[:gp]
    v_refs = refs[gp:2 * gp]
    o_ref = refs[2 * gp]
    m_sc, l_sc, acc_sc = refs[2 * gp + 1:]
    g = pl.program_id(1)
    nr = nq * N_FOX_HEADS

    @pl.when(g == 0)
    def _():
        m_sc[...] = jnp.full(m_sc.shape, NEG_INIT, F32)
        l_sc[...] = jnp.zeros(l_sc.shape, F32)
        acc_sc[...] = jnp.zeros(acc_sc.shape, F32)

    qbd = qbd_ref[0]

    def update(s_list, vt_list):
        m_prev = m_sc[...]
        m_cur = s_list[0].max(axis=-1, keepdims=True)
        for s in s_list[1:]:
            m_cur = jnp.maximum(m_cur, s.max(axis=-1, keepdims=True))
        m_new = jnp.maximum(m_prev, m_cur)
        alpha = jnp.exp(m_prev - m_new)
        l = alpha * l_sc[...]
        acc = alpha * acc_sc[...]
        for s, vt in zip(s_list, vt_list):
            p = jnp.exp(s - m_new)
            l = l + jnp.sum(p, axis=-1, keepdims=True)
            acc = acc + _dot_nt(p.astype(BF16), vt())
        m_sc[...] = m_new
        l_sc[...] = l
        acc_sc[...] = acc

    cn = cn_ref[0]
    s_list, vt_list = [], []
    for i in range(gp):
        kt = k_refs[i][0, 0].reshape(FOX_W, 128).astype(BF16)
        bias = jnp.concatenate([bias_ref[0, i]] * nq, axis=0)
        s_list.append(_dot(qbd, kt) + (bias + cn))
        vt_list.append(lambda i=i: v_refs[i][0, 0].reshape(FOX_W, 128).astype(BF16))
    update(s_list, vt_list)

    @pl.when(g == ng - 1)
    def _():
        s_new = _dot(qbd, knew_ref[0].astype(BF16)) + bnew_ref[0]
        s_new = jnp.where(bnew_ref[0] > 0.5 * MASK_VALUE, s_new, MASK_VALUE)
        update([s_new], [lambda: vnew_ref[0].astype(BF16)])
        o = acc_sc[...] / l_sc[...]
        rowh = lax.broadcasted_iota(jnp.int32, (nr, FOX_W), 0) % N_FOX_HEADS
        colh = lax.broadcasted_iota(jnp.int32, (nr, FOX_W), 1) // FOX_HEAD_DIM
        o = jnp.where(rowh == colh, o, 0.0)
        ms = jnp.sum(o * o, axis=-1, keepdims=True) * (1.0 / FOX_HEAD_DIM)
        o = o * lax.rsqrt(ms + EPS) * g_ref[...]
        o_ref[0] = jnp.sum(o.reshape(nq, N_FOX_HEADS, FOX_W), axis=1)


def _decode(page_table, qbd, cn, bias, knew, vnew, bnew, g, kt_cache, vt_cache, layer):
    bs, n_pages = page_table.shape
    nr = qbd.shape[1]
    nq = nr // N_FOX_HEADS
    gp = PAGES_PER_STEP
    ng = n_pages // gp
    page = lambda i: pl.BlockSpec(
        (1, 1, N_FOX_HEADS, FOX_HEAD_DIM, 128),
        lambda b, s, pt, i=i: (layer, pt[b, s * gp + i], 0, 0, 0))
    in_specs = [pl.BlockSpec((1, nr, FOX_W), lambda b, s, pt: (b, 0, 0)),
                pl.BlockSpec((1, nr, 1), lambda b, s, pt: (b, 0, 0)),
                pl.BlockSpec((1, gp, N_FOX_HEADS, 128), lambda b, s, pt: (b, s, 0, 0)),
                pl.BlockSpec((1, FOX_W, 128), lambda b, s, pt: (b, 0, 0)),
                pl.BlockSpec((1, FOX_W, 128), lambda b, s, pt: (b, 0, 0)),
                pl.BlockSpec((1, nr, 128), lambda b, s, pt: (b, 0, 0)),
                pl.BlockSpec((1, FOX_W), lambda b, s, pt: (0, 0))]
    in_specs += [page(i) for i in range(gp)] * 2
    grid_spec = pltpu.PrefetchScalarGridSpec(
        num_scalar_prefetch=1, grid=(bs, ng), in_specs=in_specs,
        out_specs=pl.BlockSpec((1, nq, FOX_W), lambda b, s, pt: (b, 0, 0)),
        scratch_shapes=[pltpu.VMEM((nr, 1), F32), pltpu.VMEM((nr, 1), F32),
                        pltpu.VMEM((nr, FOX_W), F32)])
    return pl.pallas_call(
        functools.partial(_decode_kernel, gp=gp, ng=ng, nq=nq),
        grid_spec=grid_spec,
        out_shape=jax.ShapeDtypeStruct((bs, nq, FOX_W), F32),
        compiler_params=_cparams("arbitrary", "arbitrary"),
        name="decode",
    )(page_table, qbd, cn, bias, knew, vnew, bnew, g, *([kt_cache] * gp), *([vt_cache] * gp))


def _hgrn_tables(L):
    nlev = int(np.log2(L))
    rows = []
    t = np.arange(L)[:, None]
    r = np.arange(L)[None, :]
    for n in range(nlev):
        h = L >> (n + 1)
        mid = (t // (2 * h)) * (2 * h) + h
        up = t >= mid
        rows.append(np.where(up, (r >= mid) & (r <= t), (r > t) & (r < mid)))
    rows.append(r <= t)
    rows.append(r > t)
    m_all = np.concatenate(rows, axis=0).astype(np.float32)
    masks = []
    for n in range(nlev):
        h = L >> (n + 1)
        masks.append((t // (2 * h)) == (r // (2 * h)))
    masks.append(t == r)
    return m_all, np.stack(masks).astype(np.float32), nlev


def _hgrn_kernel(hq_ref, hf_ref, hi_ref, hg_ref, lb_ref, g_ref, s0_ref, mall_ref, mask_ref,
                 o_ref, sout_ref, st_sc, *, L, nlev, t_valid):
    t_pad = hq_ref.shape[1]
    lb = lb_ref[...]
    st_sc[...] = s0_ref[0, 0].T
    rowi = lax.broadcasted_iota(jnp.int32, (L, HGRN_HEAD_DIM), 0)

    def chunk(c, carry):
        start = pl.multiple_of(c * L, L)
        rows = pl.ds(start, L)
        q = hq_ref[0, rows, :]
        f = lb + (1.0 - lb) * _sigmoid(hf_ref[0, rows, :])
        lf = jnp.log(jnp.maximum(f, TINY))
        kk = 1.0 - f
        if t_valid < t_pad:
            valid = (rowi + start) < t_valid
            lf = jnp.where(valid, lf, 0.0)
            kk = jnp.where(valid, kk, 0.0)
        vv = _silu(hi_ref[0, rows, :]).astype(BF16)
        p0 = lf.astype(BF16)
        r1 = lf - p0.astype(F32)
        p1 = r1.astype(BF16)
        p2 = (r1 - p1.astype(F32)).astype(BF16)
        res = _dot(mall_ref[...], jnp.concatenate([p0, p1, p2], axis=1))
        e = jnp.exp(res[:, :128] + res[:, 128:256] + res[:, 256:])
        a = mask_ref[nlev] * _dot_nt(q.astype(BF16), kk.astype(BF16))
        for n in range(nlev):
            h = L >> (n + 1)
            up = (rowi & h) != 0
            en = e[n * L:(n + 1) * L]
            qd = jnp.where(up, q * en, 0.0).astype(BF16)
            kd = jnp.where(up, 0.0, kk * en).astype(BF16)
            a = a + mask_ref[n] * _dot_nt(qd, kd)
        eb = e[nlev * L:(nlev + 1) * L]
        er = e[(nlev + 1) * L:]
        st = st_sc[...]
        o = _dot(a.astype(BF16), vv) + _dot_nt((q * eb).astype(BF16), st.astype(BF16))
        st_sc[...] = st * eb[L - 1:L, :] + _dot_tn(vv, (kk * er).astype(BF16))
        o = _rms(o) * g_ref[...] * _silu(hg_ref[0, rows, :])
        o_ref[0, rows, :] = o.astype(o_ref.dtype)
        return carry

    lax.fori_loop(0, t_pad // L, chunk, 0)
    sout_ref[0, 0] = st_sc[...].T


def _hgrn(hq, hf, hi, hg, lb, g, s0, t_valid):
    b, t_pad, _ = hq.shape
    L = HGRN_CHUNK
    m_all, masks, nlev = _hgrn_tables(L)
    act = pl.BlockSpec((1, t_pad, 128), lambda i, h: (i, 0, h))
    vec = pl.BlockSpec((1, 128), lambda i, h: (0, h))
    st = pl.BlockSpec((1, 1, 128, 128), lambda i, h: (i, h, 0, 0))
    return pl.pallas_call(
        functools.partial(_hgrn_kernel, L=L, nlev=nlev, t_valid=t_valid),
        grid=(b, N_HGRN_HEADS),
        in_specs=[act, act, act, act, vec, vec, st,
                  pl.BlockSpec(m_all.shape, lambda i, h: (0, 0)),
                  pl.BlockSpec(masks.shape, lambda i, h: (0, 0, 0))],
        out_specs=[act, st],
        out_shape=[jax.ShapeDtypeStruct((b, t_pad, HGRN_W), BF16),
                   jax.ShapeDtypeStruct((b, N_HGRN_HEADS, 128, 128), F32)],
        scratch_shapes=[pltpu.VMEM((128, 128), F32)],
        compiler_params=_cparams("arbitrary", "arbitrary"),
        name="hgrn",
    )(hq, hf, hi, hg, lb, g, s0, jnp.asarray(m_all, BF16), jnp.asarray(masks))


def _outproj_kernel(fo_ref, ho_ref, x_ref, gate_ref, sc_ref, sh_ref, g_ref, w_ref, x1_ref, h2_ref):
    mixed = _dot(fo_ref[0], w_ref[:FOX_W, :]) + _dot(ho_ref[0], w_ref[FOX_W:, :])
    x1 = x_ref[0] + gate_ref[0] * mixed
    x1_ref[0] = x1
    h2 = _rms(x1) * g_ref[...]
    h2_ref[0] = (h2 * (1.0 + sc_ref[0]) + sh_ref[0]).astype(BF16)


def _outproj(fo, ho, x, gate, sc, sh, g, w, tm):
    bx, tx, d = x.shape
    per_row = gate.shape[1] != 1
    mod_spec = pl.BlockSpec((1, tm if per_row else 1, d),
                            (lambda b, i: (b, i, 0)) if per_row else (lambda b, i: (b, 0, 0)))
    row = lambda c: pl.BlockSpec((1, tm, c), lambda b, i: (b, i, 0))
    return pl.pallas_call(
        _outproj_kernel,
        grid=(bx, tx // tm),
        in_specs=[row(FOX_W), row(HGRN_W), row(d), mod_spec, mod_spec, mod_spec,
                  pl.BlockSpec((1, d), lambda b, i: (0, 0)),
                  pl.BlockSpec(w.shape, lambda b, i: (0, 0))],
        out_specs=[row(d), row(d)],
        out_shape=[jax.ShapeDtypeStruct((bx, tx, d), F32), jax.ShapeDtypeStruct((bx, tx, d), BF16)],
        compiler_params=_cparams("arbitrary", "arbitrary"),
        name="outproj",
    )(fo, ho, x, gate, sc, sh, g, w)


def _ffn_kernel(*refs, tm, nj, period, final):
    if period is None:
        (h2_ref, x1_ref, gate_ref, wa_ref, wb_ref, cw_ref, cb_ref, wd_ref, gf_ref,
         xo_ref, cs_ref, acc_sc, halo_sc) = refs
    else:
        (h2_ref, x1_ref, gate_ref, wa_ref, wb_ref, cw_ref, cb_ref, wd_ref, gf_ref, p1_ref, p2_ref,
         xo_ref, cs_ref, acc_sc) = refs
    i = pl.program_id(1)
    j = pl.program_id(2)
    h2 = h2_ref[0]
    a = _dot(h2, wa_ref[...])
    bb = _dot(h2, wb_ref[...])
    tn = a.shape[1]
    rowi = lax.broadcasted_iota(jnp.int32, (tm, tn), 0)
    a1 = pltpu.roll(a, 1, axis=0)
    a2 = pltpu.roll(a, 2, axis=0)
    if period is None:
        @pl.when(i == 0)
        def _():
            halo_sc[j] = jnp.zeros((8, tn), F32)
        hl = halo_sc[j]
        a1 = jnp.where(rowi == 0, hl[7:8, :], a1)
        a2 = jnp.where(rowi == 0, hl[6:7, :], jnp.where(rowi == 1, hl[7:8, :], a2))
        halo_sc[j] = a[tm - 8:, :]
        cs_ref[0, 0] = a[tm - 2:, :]
    else:
        tpos = rowi % period
        a1 = jnp.where(tpos >= 1, a1, p1_ref[0])
        a2 = jnp.where(tpos >= 2, a2, p2_ref[0])
        cs_ref[0] = a
    cw = cw_ref[...]
    a_conv = cb_ref[...] + cw[0:1, :] * a2 + cw[1:2, :] * a1 + cw[2:3, :] * a
    gated = (_gelu_tanh(a_conv) * bb).astype(BF16)
    contrib = _dot(gated, wd_ref[...])

    @pl.when(j == 0)
    def _():
        acc_sc[...] = contrib

    @pl.when(j > 0)
    def _():
        acc_sc[...] += contrib

    @pl.when(j == nj - 1)
    def _():
        x2 = x1_ref[0] + gate_ref[0] * acc_sc[...]
        if final:
            x2 = _rms(x2) * gf_ref[...]
        xo_ref[0] = x2


def _ffn(h2, x1, gate, w_up, cw, cb, wd, gf, tm, final, prev=None):
    bx, tx, d = x1.shape
    tn = 256
    nj = D_FF // tn
    per_row = gate.shape[1] != 1
    mod_spec = pl.BlockSpec((1, tm if per_row else 1, d),
                            (lambda b, i, j: (b, i, 0)) if per_row else (lambda b, i, j: (b, 0, 0)))
    row = pl.BlockSpec((1, tm, d), lambda b, i, j: (b, i, 0))
    in_specs = [row, row, mod_spec,
                pl.BlockSpec((d, tn), lambda b, i, j: (0, j)),
                pl.BlockSpec((d, tn), lambda b, i, j: (0, j + nj)),
                pl.BlockSpec((3, tn), lambda b, i, j: (0, j)),
                pl.BlockSpec((1, tn), lambda b, i, j: (0, j)),
                pl.BlockSpec((tn, d), lambda b, i, j: (j, 0)),
                pl.BlockSpec((1, d), lambda b, i, j: (0, 0))]
    args = [h2, x1, gate, w_up, w_up, cw, cb, wd, gf]
    scratch = [pltpu.VMEM((tm, d), F32)]
    if prev is None:
        period = None
        cs_spec = pl.BlockSpec((1, 1, 2, tn), lambda b, i, j: (b, i, 0, j))
        cs_shape = jax.ShapeDtypeStruct((bx, tx // tm, 2, D_FF), F32)
        scratch.append(pltpu.VMEM((nj, 8, tn), F32))
    else:
        period = prev[2]
        in_specs += [pl.BlockSpec((1, tm, tn), lambda b, i, j: (b, i, j))] * 2
        args += [prev[0], prev[1]]
        cs_spec = pl.BlockSpec((1, tm, tn), lambda b, i, j: (b, i, j))
        cs_shape = jax.ShapeDtypeStruct((bx, tx, D_FF), F32)
    return pl.pallas_call(
        functools.partial(_ffn_kernel, tm=tm, nj=nj, period=period, final=final),
        grid=(bx, tx // tm, nj),
        in_specs=in_specs,
        out_specs=[row, cs_spec],
        out_shape=[jax.ShapeDtypeStruct((bx, tx, d), F32), cs_shape],
        scratch_shapes=scratch,
        compiler_params=_cparams("arbitrary", "arbitrary", "arbitrary"),
        name="ffn",
    )(*args)


def kernel(x_prompt, x_sample, c_prompt, c_sample, cache_k, cache_v, cache_logf, page_table, state_hgrn, state_conv, norm_g, final_norm_g, w_ada, b_ada, w_in, fox_fbias, fox_onorm_g, hgrn_lb_logits, hgrn_onorm_g, w_out, w_up, conv_w, conv_b, w_down):
    depth = w_in.shape[0]
    bp, tp, d = x_prompt.shape
    bs, ts, _ = x_sample.shape
    ns = bs * ts
    n_pages = page_table.shape[1]

    p = jax.nn.softmax(hgrn_lb_logits.astype(F32), axis=0)
    lb_all = jnp.maximum(jnp.cumsum(p, axis=0) - p[0:1], 0.0)

    kt_cache = cache_k.transpose(0, 1, 3, 4, 2)
    vt_cache = cache_v.transpose(0, 1, 3, 4, 2)
    lft_cache = cache_logf.transpose(0, 1, 3, 2)

    mod = _ada(jnp.concatenate([c_prompt, c_sample], axis=0), w_ada, b_ada)
    gf = final_norm_g.reshape(1, d)

    xp = x_prompt
    xs = x_sample.reshape(1, ns, d)
    outs_p = [[] for _ in range(5)]
    outs_s = [[] for _ in range(5)]
    for l in range(depth):
        mp = mod[l, :bp].reshape(bp, 6, 1, d)
        ms = jnp.repeat(mod[l, bp:].reshape(bs, 6, d), ts, axis=0).reshape(1, ns, 6, d)
        ms = jnp.moveaxis(ms, 2, 0)
        mods_p = [mp[:, j] for j in range(6)]
        mods_s = [ms[j] for j in range(6)]

        wl = w_in[l]
        wm = jnp.concatenate([wl[:, :3 * FOX_W], wl[:, 3 * FOX_W + N_FOX_HEADS:]], axis=1).astype(BF16)
        wff = wl[:, 3 * FOX_W:3 * FOX_W + N_FOX_HEADS]
        wf = jnp.pad(wff, ((0, 0), (0, 128 - N_FOX_HEADS))).astype(BF16)
        wft = wff.T.astype(BF16)
        fb = fox_fbias[l].reshape(1, N_FOX_HEADS)
        fbt = fox_fbias[l].reshape(N_FOX_HEADS, 1)
        g1 = norm_g[l, 0].reshape(1, d)
        g2 = norm_g[l, 1].reshape(1, d)
        fog = fox_onorm_g[l].reshape(1, FOX_W)
        hog = hgrn_onorm_g[l].reshape(1, HGRN_W)
        lb = lb_all[l].reshape(1, HGRN_W)
        wo = w_out[l].astype(BF16)
        wu = w_up[l].astype(BF16)
        wd = w_down[l].astype(BF16)
        cw = conv_w[l]
        cb = conv_b[l].reshape(1, D_FF)
        final = l == depth - 1

        (q, k, v, kb, vb, lf, lft, hq, hf, hi, hg) = _inproj(
            xp, mods_p[1], mods_p[0], g1, wm, wf, wft, fb, fbt, tm=512)
        ccol, crow = _cumsum(lf, lft)
        fo = _fox_prompt(q, kb, vb, ccol, crow, fog)
        s0p = jnp.zeros((bp, N_HGRN_HEADS, HGRN_HEAD_DIM, HGRN_HEAD_DIM), F32)
        ho, s_new = _hgrn(hq, hf, hi, hg, lb, hog, s0p, t_valid=tp)
        x1, h2 = _outproj(fo, ho, xp, mods_p[2], mods_p[4], mods_p[3], g2, wo, tm=512)
        xp, conv_p = _ffn(h2, x1, mods_p[5], wu, cw, cb, wd, gf, tm=1024, final=final)
        outs_p[0].append(k.reshape(bp, tp, N_FOX_HEADS, FOX_HEAD_DIM))
        outs_p[1].append(v.reshape(bp, tp, N_FOX_HEADS, FOX_HEAD_DIM))
        outs_p[2].append(lft.transpose(0, 2, 1))
        outs_p[3].append(s_new)
        outs_p[4].append(conv_p[:, -1])

        (q, k, v, kb, vb, lf, lft, hq, hf, hi, hg) = _inproj(
            xs, mods_s[1], mods_s[0], g1, wm, wf, wft, fb, fbt, tm=ns)
        k4 = k.reshape(bs, ts, N_FOX_HEADS, FOX_HEAD_DIM)
        v4 = v.reshape(bs, ts, N_FOX_HEADS, FOX_HEAD_DIM)
        lf3 = lf.reshape(bs, ts, N_FOX_HEADS)
        q4 = q.reshape(bs, ts, N_FOX_HEADS, FOX_HEAD_DIM)
        eye = jnp.eye(N_FOX_HEADS, dtype=BF16)
        qbd = (q4[:, :, :, None, :] * eye[None, None, :, :, None]).reshape(bs, ts * N_FOX_HEADS, FOX_W)
        cum = jnp.cumsum(lf3, axis=1)
        cn = cum.reshape(bs, ts * N_FOX_HEADS, 1)
        tok = jnp.arange(128)
        diff = cum[:, :, None, :] - jnp.pad(cum, ((0, 0), (0, 128 - ts), (0, 0)))[:, None, :, :]
        ok = (tok[None, :] <= jnp.arange(ts)[:, None]) & (tok[None, :] < ts)
        bnew = jnp.where(ok[None, :, :, None], diff, MASK_VALUE)
        bnew = bnew.transpose(0, 1, 3, 2).reshape(bs, ts * N_FOX_HEADS, 128)
        pad_t = lambda a: jnp.pad(a.transpose(0, 2, 3, 1).reshape(bs, FOX_W, ts),
                                  ((0, 0), (0, 0), (0, 128 - ts)))
        bias = _pbias(page_table, lft_cache, l)
        fo_s = _decode(page_table, qbd, cn, bias, pad_t(k4), pad_t(v4), bnew, fog,
                       kt_cache, vt_cache, l)
        fo_s = fo_s.reshape(1, ns, FOX_W).astype(BF16)

        t_pad = HGRN_CHUNK
        padh = lambda a: jnp.pad(a.reshape(bs, ts, HGRN_W), ((0, 0), (0, t_pad - ts), (0, 0)))
        ho_s, s_new_s = _hgrn(padh(hq), padh(hf), padh(hi), padh(hg), lb, hog, state_hgrn[l],
                              t_valid=ts)
        ho_s = ho_s[:, :ts].reshape(1, ns, HGRN_W)
        x1, h2 = _outproj(fo_s, ho_s, xs, mods_s[2], mods_s[4], mods_s[3], g2, wo, tm=ns)
        buf = state_conv[l]
        zrow = jnp.zeros((bs, 1, D_FF), F32)
        prev1 = jnp.concatenate([buf[:, 1:2], zrow, zrow, zrow], axis=1).reshape(1, ns, D_FF)
        prev2 = jnp.concatenate([buf[:, 0:1], buf[:, 1:2], zrow, zrow], axis=1).reshape(1, ns, D_FF)
        xs, a_full = _ffn(h2, x1, mods_s[5], wu, cw, cb, wd, gf, tm=ns, final=final,
                          prev=(prev1, prev2, ts))
        outs_s[0].append(k4)
        outs_s[1].append(v4)
        outs_s[2].append(lf3)
        outs_s[3].append(s_new_s)
        outs_s[4].append(a_full.reshape(bs, ts, D_FF)[:, ts - 2:])

    return (xp, xs.reshape(bs, ts, d),
            *[jnp.stack(o) for o in outs_p],
            *[jnp.stack(o) for o in outs_s])
```

```python
import functools

import numpy as np
import jax
import jax.numpy as jnp
from jax import lax
from jax.experimental import pallas as pl
from jax.experimental.pallas import tpu as pltpu

F32 = jnp.float32
BF16 = jnp.bfloat16
HIGHEST = lax.Precision.HIGHEST

D_MODEL = 1024
FOX_W = 512
FOX_HEAD_DIM = 64
N_FOX_HEADS = 8
HGRN_W = 512
HGRN_HEAD_DIM = 128
N_HGRN_HEADS = 4
D_FF = 2816
EPS = 1e-6
FOX_SCALE = FOX_HEAD_DIM ** -0.5
MASK_VALUE = -1e30
TINY = 1e-30
NEG_INIT = -1e30
LANES = 128

HGRN_CHUNK = 64
PAGES_PER_STEP = 8
FOX_BLOCK = 512
VMEM_LIMIT = 56 * 1024 * 1024


def _cparams(*sem):
    return pltpu.CompilerParams(dimension_semantics=sem, vmem_limit_bytes=VMEM_LIMIT)


def _sigmoid(x):
    return 1.0 / (1.0 + jnp.exp(-x))


def _silu(x):
    return x * _sigmoid(x)


def _log_sigmoid(x):
    return jnp.minimum(x, 0.0) - jnp.log(1.0 + jnp.exp(-jnp.abs(x)))


def _gelu_tanh(x):
    c = np.float32(np.sqrt(2.0 / np.pi))
    return 0.5 * x * (1.0 + jnp.tanh(c * (x + 0.044715 * (x * x * x))))


def _rms(x):
    return x * lax.rsqrt(jnp.mean(x * x, axis=-1, keepdims=True) + EPS)


def _dot(a, b):
    return jnp.dot(a, b, preferred_element_type=F32)


def _dot_nt(a, b):
    return lax.dot_general(a, b, (((1,), (1,)), ((), ())), preferred_element_type=F32)


def _dot_tn(a, b):
    return lax.dot_general(a, b, (((0,), (0,)), ((), ())), preferred_element_type=F32)


def _split3(x):
    p0 = x.astype(BF16)
    r1 = x - p0.astype(F32)
    p1 = r1.astype(BF16)
    p2 = (r1 - p1.astype(F32)).astype(BF16)
    return p0, p1, p2


def _ada_kernel(c_ref, w_ref, b_ref, o_ref):
    a = _silu(c_ref[...])
    o_ref[0] = jnp.dot(a, w_ref[0], precision=HIGHEST, preferred_element_type=F32) + b_ref[0]


def _ada(c_all, w_ada, b_ada):
    depth, d, n = w_ada.shape
    nc = c_all.shape[0]
    tn = 1536
    return pl.pallas_call(
        _ada_kernel,
        grid=(depth, n // tn),
        in_specs=[pl.BlockSpec((nc, d), lambda l, j: (0, 0)),
                  pl.BlockSpec((1, d, tn), lambda l, j: (l, 0, j)),
                  pl.BlockSpec((1, 1, tn), lambda l, j: (l, 0, j))],
        out_specs=pl.BlockSpec((1, nc, tn), lambda l, j: (l, 0, j)),
        out_shape=jax.ShapeDtypeStruct((depth, nc, n), F32),
        compiler_params=_cparams("arbitrary", "arbitrary"),
        name="ada",
    )(c_all, w_ada, b_ada.reshape(depth, 1, n))


def _inproj_kernel(x_ref, sc_ref, sh_ref, g_ref, wn_ref, wkv_ref, wft_ref, fbt_ref,
                   q_ref, kt_ref, vt_ref, ktb_ref, vtb_ref, lft_ref,
                   hq_ref, hf_ref, hi_ref, hg_ref):
    h = _rms(x_ref[0]) * g_ref[...]
    h = h * (1.0 + sc_ref[0]) + sh_ref[0]
    hb = h.astype(BF16)

    def grp(i):
        return _dot(hb, wn_ref[:, i * 512:(i + 1) * 512])

    q_ref[0] = (grp(0) * FOX_SCALE).astype(BF16)
    hq_ref[0] = grp(1)
    hf_ref[0] = grp(2)
    hi_ref[0] = grp(3)
    hg_ref[0] = grp(4)
    kvt = _dot_nt(wkv_ref[...], hb)
    kt_ref[0] = kvt[:FOX_W]
    ktb_ref[0] = kvt[:FOX_W].astype(BF16)
    vt_ref[0] = kvt[FOX_W:]
    vtb_ref[0] = kvt[FOX_W:].astype(BF16)
    lft_ref[0] = _log_sigmoid(_dot_nt(wft_ref[...], hb) + fbt_ref[...])


def _inproj(x, sc, sh, g, wn, wkv, wft, fbt, tm):
    bx, tx, d = x.shape
    per_row = sc.shape[1] != 1
    mod_spec = pl.BlockSpec((1, tm if per_row else 1, d),
                            (lambda b, i: (b, i, 0)) if per_row else (lambda b, i: (b, 0, 0)))
    row = lambda c: pl.BlockSpec((1, tm, c), lambda b, i: (b, i, 0))
    col = lambda r: pl.BlockSpec((1, r, tm), lambda b, i: (b, 0, i))
    const = lambda a: pl.BlockSpec(a.shape, lambda b, i: (0,) * a.ndim)
    sds = jax.ShapeDtypeStruct
    outs = [sds((bx, tx, FOX_W), BF16),
            sds((bx, FOX_W, tx), F32), sds((bx, FOX_W, tx), F32),
            sds((bx, FOX_W, tx), BF16), sds((bx, FOX_W, tx), BF16),
            sds((bx, N_FOX_HEADS, tx), F32),
            sds((bx, tx, HGRN_W), F32), sds((bx, tx, HGRN_W), F32), sds((bx, tx, HGRN_W), F32),
            sds((bx, tx, HGRN_W), F32)]
    out_specs = [row(FOX_W)] + [col(FOX_W)] * 4 + [col(N_FOX_HEADS)] + [row(HGRN_W)] * 4
    return pl.pallas_call(
        _inproj_kernel,
        grid=(bx, tx // tm),
        in_specs=[row(d), mod_spec, mod_spec, const(g), const(wn), const(wkv), const(wft),
                  const(fbt)],
        out_specs=out_specs,
        out_shape=outs,
        compiler_params=_cparams("arbitrary", "arbitrary"),
        name="inproj",
    )(x, sc, sh, g, wn, wkv, wft, fbt)


def _cumsum_kernel(lft_ref, triu_ref, crow_ref, *, blk):
    nh, t = lft_ref.shape[1], lft_ref.shape[2]
    carry = jnp.zeros((nh, 1), F32)
    for i in range(t // blk):
        sl = slice(i * blk, (i + 1) * blk)
        r = jnp.dot(lft_ref[0, :, sl], triu_ref[...], precision=HIGHEST,
                    preferred_element_type=F32) + carry
        crow_ref[0, :, sl] = r
        carry = r[:, blk - 1:blk]


def _cumsum(lft):
    b, nh, t = lft.shape
    blk = 256
    triu = jnp.asarray(np.triu(np.ones((blk, blk), np.float32)))
    return pl.pallas_call(
        functools.partial(_cumsum_kernel, blk=blk),
        grid=(b,),
        in_specs=[pl.BlockSpec((1, nh, t), lambda i: (i, 0, 0)),
                  pl.BlockSpec((blk, blk), lambda i: (0, 0))],
        out_specs=pl.BlockSpec((1, nh, t), lambda i: (i, 0, 0)),
        out_shape=jax.ShapeDtypeStruct((b, nh, t), F32),
        compiler_params=_cparams("arbitrary"),
        name="cumsum",
    )(lft, triu)


def _fox_kernel(q_ref, kt_ref, vt_ref, crow_ref, g_ref, o_ref, m_sc, l_sc, acc_sc, *, tb):
    qi = pl.program_id(2)
    q = q_ref[0]
    lane = lax.broadcasted_iota(jnp.int32, (tb, LANES), 1)
    lo = lane < FOX_HEAD_DIM
    qs = (jnp.where(lo, q, jnp.zeros_like(q)), jnp.where(lo, jnp.zeros_like(q), q))
    nc = tb // LANES
    m_sc[...] = jnp.full(m_sc.shape, NEG_INIT, F32)
    l_sc[...] = jnp.zeros(l_sc.shape, F32)
    acc_sc[...] = jnp.zeros(acc_sc.shape, F32)

    def step(kj, masked):
        start = pl.multiple_of(kj * tb, tb)
        kt = kt_ref[0, :, pl.ds(start, tb)]
        vt = vt_ref[0, :, pl.ds(start, tb)]
        for e in range(2):
            ck = crow_ref[0, 0, pl.ds(e, 1), pl.ds(start, tb)]
            s = _dot(qs[e], kt) - ck
            parts = [s[:, c * LANES:(c + 1) * LANES] for c in range(nc)]
            if masked:
                rows = lax.broadcasted_iota(jnp.int32, (tb, LANES), 0)
                cols = lax.broadcasted_iota(jnp.int32, (tb, LANES), 1)
                parts = [jnp.where(rows >= cols + c * LANES, p, MASK_VALUE)
                         for c, p in enumerate(parts)]
            mt = parts[0]
            for p in parts[1:]:
                mt = jnp.maximum(mt, p)
            m_prev = m_sc[e]
            m_new = jnp.maximum(m_prev, jnp.max(mt, axis=-1, keepdims=True))
            alpha = jnp.exp(m_prev - m_new)
            ps = [jnp.exp(p - m_new) for p in parts]
            lsum = ps[0]
            for p in ps[1:]:
                lsum = lsum + p
            l_sc[e] = alpha * l_sc[e] + lsum
            pb = jnp.concatenate([p.astype(BF16) for p in ps], axis=1)
            acc_sc[e] = alpha * acc_sc[e] + _dot_nt(pb, vt)
            m_sc[e] = m_new

    def body(kj, carry):
        step(kj, False)
        return carry

    lax.fori_loop(0, qi, body, 0)
    step(qi, True)
    o0 = acc_sc[0] / jnp.sum(l_sc[0], axis=-1, keepdims=True)
    o1 = acc_sc[1] / jnp.sum(l_sc[1], axis=-1, keepdims=True)
    o = jnp.where(lo, o0, o1)
    sq = o * o
    s_lo = jnp.sum(jnp.where(lo, sq, 0.0), axis=-1, keepdims=True)
    s_all = jnp.sum(sq, axis=-1, keepdims=True)
    ms = jnp.where(lo, s_lo, s_all - s_lo) * (1.0 / FOX_HEAD_DIM)
    o_ref[0] = (o * lax.rsqrt(ms + EPS) * g_ref[...]).astype(o_ref.dtype)


def _fox_prompt(q, ktb, vtb, crow, g):
    b, t, _ = q.shape
    tb = min(FOX_BLOCK, t)
    nhp = N_FOX_HEADS // 2
    crow4 = crow.reshape(b, nhp, 2, t)
    return pl.pallas_call(
        functools.partial(_fox_kernel, tb=tb),
        grid=(b, nhp, t // tb),
        in_specs=[pl.BlockSpec((1, tb, LANES), lambda i, h, j: (i, j, h)),
                  pl.BlockSpec((1, LANES, t), lambda i, h, j: (i, h, 0)),
                  pl.BlockSpec((1, LANES, t), lambda i, h, j: (i, h, 0)),
                  pl.BlockSpec((1, 1, 2, t), lambda i, h, j: (i, h, 0, 0)),
                  pl.BlockSpec((1, LANES), lambda i, h, j: (0, h))],
        out_specs=pl.BlockSpec((1, tb, LANES), lambda i, h, j: (i, j, h)),
        out_shape=jax.ShapeDtypeStruct((b, t, FOX_W), BF16),
        scratch_shapes=[pltpu.VMEM((2, tb, LANES), F32), pltpu.VMEM((2, tb, LANES), F32),
                        pltpu.VMEM((2, tb, LANES), F32)],
        compiler_params=_cparams("arbitrary", "arbitrary", "arbitrary"),
        name="fox",
    )(q, ktb, vtb, crow4, g)


def _decode_kernel(pt_ref, qbd_ref, cn_ref, knew_ref, vnew_ref, bnew_ref, g_ref, u_ref, *refs,
                   gp, ng, nq):
    k_refs = refs[:gp]
    v_refs = refs[gp:2 * gp]
    lf_refs = refs[2 * gp:3 * gp]
    o_ref = refs[3 * gp]
    m_sc, l_sc, acc_sc, carry_sc = refs[3 * gp + 1:]
    g = pl.program_id(1)
    nr = nq * N_FOX_HEADS

    @pl.when(g == 0)
    def _():
        m_sc[...] = jnp.full(m_sc.shape, NEG_INIT, F32)
        l_sc[...] = jnp.zeros(l_sc.shape, F32)
        acc_sc[...] = jnp.zeros(acc_sc.shape, F32)
        carry_sc[...] = jnp.zeros(carry_sc.shape, F32)

    qbd = qbd_ref[0]

    def update(s, vt):
        m_prev = m_sc[...]
        m_new = jnp.maximum(m_prev, jnp.max(s, axis=-1, keepdims=True))
        alpha = jnp.exp(m_prev - m_new)
        p = jnp.exp(s - m_new)
        l_sc[...] = alpha * l_sc[...] + jnp.sum(p, axis=-1, keepdims=True)
        acc_sc[...] = alpha * acc_sc[...] + _dot_nt(p.astype(BF16), vt)
        m_sc[...] = m_new

    lf_all = jnp.concatenate([r[0, 0] for r in lf_refs], axis=0)
    r = _dot(jnp.concatenate(_split3(lf_all), axis=0), u_ref[...])
    r = r[:8 * gp] + r[8 * gp:16 * gp] + r[16 * gp:]
    carry = carry_sc[...]
    biases = [None] * gp
    for i in range(gp - 1, -1, -1):
        sl = slice(i * N_FOX_HEADS, (i + 1) * N_FOX_HEADS)
        biases[i] = jnp.concatenate([r[sl, :LANES] + carry] * nq, axis=0)
        carry = carry + r[sl, LANES:]
    carry_sc[...] = carry
    bias = jnp.concatenate(biases, axis=1) + cn_ref[0]
    kt = jnp.concatenate([r[0, 0].reshape(FOX_W, LANES).astype(BF16) for r in k_refs], axis=1)
    vt = jnp.concatenate([r[0, 0].reshape(FOX_W, LANES).astype(BF16) for r in v_refs], axis=1)
    update(_dot(qbd, kt) + bias, vt)

    @pl.when(g == ng - 1)
    def _():
        s_new = _dot(qbd, knew_ref[0].astype(BF16)) + bnew_ref[0]
        s_new = jnp.where(bnew_ref[0] > 0.5 * MASK_VALUE, s_new, MASK_VALUE)
        update(s_new, vnew_ref[0].astype(BF16))
        o = acc_sc[...] / l_sc[...]
        rowh = lax.broadcasted_iota(jnp.int32, (nr, FOX_W), 0) % N_FOX_HEADS
        colh = lax.broadcasted_iota(jnp.int32, (nr, FOX_W), 1) // FOX_HEAD_DIM
        o = jnp.where(rowh == colh, o, 0.0)
        ms = jnp.sum(o * o, axis=-1, keepdims=True) * (1.0 / FOX_HEAD_DIM)
        o = o * lax.rsqrt(ms + EPS) * g_ref[...]
        o_ref[0] = jnp.sum(o.reshape(nq, N_FOX_HEADS, FOX_W), axis=1)


def _decode(page_table, qbd, cn, knew, vnew, bnew, g, kt_cache, vt_cache, lft_cache, layer):
    bs, n_pages = page_table.shape
    nr = qbd.shape[1]
    nq = nr // N_FOX_HEADS
    gp = min(PAGES_PER_STEP, n_pages)
    ng = n_pages // gp
    u = np.zeros((LANES, 2 * LANES), np.float32)
    u[:, :LANES] = np.tril(np.ones((LANES, LANES), np.float32), -1)
    u[:, LANES:] = 1.0
    pidx = lambda b, s, pt, i: pt[b, (ng - 1 - s) * gp + i]
    page = lambda i: pl.BlockSpec(
        (1, 1, N_FOX_HEADS, FOX_HEAD_DIM, LANES),
        lambda b, s, pt, i=i: (layer, pidx(b, s, pt, i), 0, 0, 0))
    lfpage = lambda i: pl.BlockSpec(
        (1, 1, N_FOX_HEADS, LANES), lambda b, s, pt, i=i: (layer, pidx(b, s, pt, i), 0, 0))
    in_specs = [pl.BlockSpec((1, nr, FOX_W), lambda b, s, pt: (b, 0, 0)),
                pl.BlockSpec((1, nr, 1), lambda b, s, pt: (b, 0, 0)),
                pl.BlockSpec((1, FOX_W, LANES), lambda b, s, pt: (b, 0, 0)),
                pl.BlockSpec((1, FOX_W, LANES), lambda b, s, pt: (b, 0, 0)),
                pl.BlockSpec((1, nr, LANES), lambda b, s, pt: (b, 0, 0)),
                pl.BlockSpec((1, FOX_W), lambda b, s, pt: (0, 0)),
                pl.BlockSpec((LANES, 2 * LANES), lambda b, s, pt: (0, 0))]
    in_specs += [page(i) for i in range(gp)] * 2 + [lfpage(i) for i in range(gp)]
    grid_spec = pltpu.PrefetchScalarGridSpec(
        num_scalar_prefetch=1, grid=(bs, ng), in_specs=in_specs,
        out_specs=pl.BlockSpec((1, nq, FOX_W), lambda b, s, pt: (b, 0, 0)),
        scratch_shapes=[pltpu.VMEM((nr, 1), F32), pltpu.VMEM((nr, 1), F32),
                        pltpu.VMEM((nr, FOX_W), F32), pltpu.VMEM((N_FOX_HEADS, LANES), F32)])
    return pl.pallas_call(
        functools.partial(_decode_kernel, gp=gp, ng=ng, nq=nq),
        grid_spec=grid_spec,
        out_shape=jax.ShapeDtypeStruct((bs, nq, FOX_W), F32),
        compiler_params=_cparams("arbitrary", "arbitrary"),
        name="decode",
    )(page_table, qbd, cn, knew, vnew, bnew, g, jnp.asarray(u, BF16),
      *([kt_cache] * gp), *([vt_cache] * gp), *([lft_cache] * gp))


def _hgrn_tables(L):
    nlev = int(np.log2(L))
    rows = []
    t = np.arange(L)[:, None]
    r = np.arange(L)[None, :]
    for n in range(nlev):
        h = L >> (n + 1)
        mid = (t // (2 * h)) * (2 * h) + h
        up = t >= mid
        rows.append(np.where(up, (r >= mid) & (r <= t), (r > t) & (r < mid)))
    rows.append(r <= t)
    rows.append(r > t)
    m_all = np.concatenate(rows, axis=0).astype(np.float32)
    masks = []
    for n in range(nlev):
        h = L >> (n + 1)
        masks.append((t // (2 * h)) == (r // (2 * h)))
    masks.append(t == r)
    return m_all, np.stack(masks).astype(np.float32), nlev


def _hgrn_kernel(hq_ref, hf_ref, hi_ref, hg_ref, lb_ref, g_ref, s0_ref, mall_ref, mask_ref,
                 o_ref, sout_ref, st_sc, *, L, nlev, t_valid):
    t_pad = hq_ref.shape[1]
    hd = HGRN_HEAD_DIM
    for hh in range(N_HGRN_HEADS):
        st_sc[hh] = s0_ref[0, hh].T
    rowi = lax.broadcasted_iota(jnp.int32, (L, hd), 0)

    def chunk(c, carry):
        start = pl.multiple_of(c * L, L)
        rows = pl.ds(start, L)
        for hh in range(N_HGRN_HEADS):
            cols = slice(hh * hd, (hh + 1) * hd)
            lb = lb_ref[:, cols]
            q = hq_ref[0, rows, cols]
            f = lb + (1.0 - lb) * _sigmoid(hf_ref[0, rows, cols])
            lf = jnp.log(jnp.maximum(f, TINY))
            kk = 1.0 - f
            if t_valid < t_pad:
                valid = (rowi + start) < t_valid
                lf = jnp.where(valid, lf, 0.0)
                kk = jnp.where(valid, kk, 0.0)
            vv = _silu(hi_ref[0, rows, cols]).astype(BF16)
            res = _dot(mall_ref[...], jnp.concatenate(_split3(lf), axis=1))
            e = jnp.exp(res[:, :hd] + res[:, hd:2 * hd] + res[:, 2 * hd:])
            a = mask_ref[nlev] * _dot_nt(q.astype(BF16), kk.astype(BF16))
            for n in range(nlev):
                up = (rowi & (L >> (n + 1))) != 0
                en = e[n * L:(n + 1) * L]
                qd = jnp.where(up, q * en, 0.0).astype(BF16)
                kd = jnp.where(up, 0.0, kk * en).astype(BF16)
                a = a + mask_ref[n] * _dot_nt(qd, kd)
            eb = e[nlev * L:(nlev + 1) * L]
            er = e[(nlev + 1) * L:]
            st = st_sc[hh]
            o = _dot(a.astype(BF16), vv) + _dot_nt((q * eb).astype(BF16), st.astype(BF16))
            st_sc[hh] = st * eb[L - 1:L, :] + _dot_tn(vv, (kk * er).astype(BF16))
            o = _rms(o) * g_ref[:, cols] * _silu(hg_ref[0, rows, cols])
            o_ref[0, rows, cols] = o.astype(o_ref.dtype)
        return carry

    lax.fori_loop(0, t_pad // L, chunk, 0)
    for hh in range(N_HGRN_HEADS):
        sout_ref[0, hh] = st_sc[hh].T


def _hgrn(hq, hf, hi, hg, lb, g, s0, t_valid):
    b, t_pad, _ = hq.shape
    L = HGRN_CHUNK
    m_all, masks, nlev = _hgrn_tables(L)
    act = pl.BlockSpec((1, t_pad, HGRN_W), lambda i: (i, 0, 0))
    vec = pl.BlockSpec((1, HGRN_W), lambda i: (0, 0))
    st = pl.BlockSpec((1, N_HGRN_HEADS, HGRN_HEAD_DIM, HGRN_HEAD_DIM), lambda i: (i, 0, 0, 0))
    return pl.pallas_call(
        functools.partial(_hgrn_kernel, L=L, nlev=nlev, t_valid=t_valid),
        grid=(b,),
        in_specs=[act, act, act, act, vec, vec, st,
                  pl.BlockSpec(m_all.shape, lambda i: (0, 0)),
                  pl.BlockSpec(masks.shape, lambda i: (0, 0, 0))],
        out_specs=[act, st],
        out_shape=[jax.ShapeDtypeStruct((b, t_pad, HGRN_W), BF16),
                   jax.ShapeDtypeStruct((b, N_HGRN_HEADS, HGRN_HEAD_DIM, HGRN_HEAD_DIM), F32)],
        scratch_shapes=[pltpu.VMEM((N_HGRN_HEADS, HGRN_HEAD_DIM, HGRN_HEAD_DIM), F32)],
        compiler_params=_cparams("arbitrary"),
        name="hgrn",
    )(hq, hf, hi, hg, lb, g, s0, jnp.asarray(m_all, BF16), jnp.asarray(masks))


def _outproj_kernel(fo_ref, ho_ref, x_ref, gate_ref, sc_ref, sh_ref, g_ref, w_ref, x1_ref, h2_ref):
    mixed = _dot(fo_ref[0], w_ref[:FOX_W, :]) + _dot(ho_ref[0], w_ref[FOX_W:, :])
    x1 = x_ref[0] + gate_ref[0] * mixed
    x1_ref[0] = x1
    h2 = _rms(x1) * g_ref[...]
    h2_ref[0] = (h2 * (1.0 + sc_ref[0]) + sh_ref[0]).astype(BF16)


def _outproj(fo, ho, x, gate, sc, sh, g, w, tm):
    bx, tx, d = x.shape
    per_row = gate.shape[1] != 1
    mod_spec = pl.BlockSpec((1, tm if per_row else 1, d),
                            (lambda b, i: (b, i, 0)) if per_row else (lambda b, i: (b, 0, 0)))
    row = lambda c: pl.BlockSpec((1, tm, c), lambda b, i: (b, i, 0))
    return pl.pallas_call(
        _outproj_kernel,
        grid=(bx, tx // tm),
        in_specs=[row(FOX_W), row(HGRN_W), row(d), mod_spec, mod_spec, mod_spec,
                  pl.BlockSpec((1, d), lambda b, i: (0, 0)),
                  pl.BlockSpec(w.shape, lambda b, i: (0, 0))],
        out_specs=[row(d), row(d)],
        out_shape=[jax.ShapeDtypeStruct((bx, tx, d), F32), jax.ShapeDtypeStruct((bx, tx, d), BF16)],
        compiler_params=_cparams("arbitrary", "arbitrary"),
        name="outproj",
    )(fo, ho, x, gate, sc, sh, g, w)


def _ffn_kernel(*refs, tm, nj, period, final):
    if period is None:
        (h2_ref, x1_ref, gate_ref, w_ref, cw_ref, cb_ref, wd_ref, gf_ref,
         xo_ref, cs_ref, gated_sc, halo_sc) = refs
    else:
        (h2_ref, x1_ref, gate_ref, w_ref, cw_ref, cb_ref, wd_ref, gf_ref, p1_ref, p2_ref,
         xo_ref, cs_ref, gated_sc) = refs
    i = pl.program_id(1)
    h2 = h2_ref[0]
    tn = D_FF // nj
    rowi = lax.broadcasted_iota(jnp.int32, (tm, tn), 0)
    if period is None:
        @pl.when(i == 0)
        def _():
            halo_sc[...] = jnp.zeros(halo_sc.shape, F32)

    for j in range(nj):
        cols = slice(j * tn, (j + 1) * tn)
        a = _dot(h2, w_ref[:, cols])
        bb = _dot(h2, w_ref[:, D_FF + j * tn:D_FF + (j + 1) * tn])
        a1 = pltpu.roll(a, 1, axis=0)
        a2 = pltpu.roll(a, 2, axis=0)
        if period is None:
            hl = halo_sc[j]
            a1 = jnp.where(rowi == 0, hl[7:8, :], a1)
            a2 = jnp.where(rowi == 0, hl[6:7, :], jnp.where(rowi == 1, hl[7:8, :], a2))
            halo_sc[j] = a[tm - 8:, :]
            cs_ref[0, 0, :, cols] = a[tm - 2:, :]
        else:
            tpos = rowi % period
            a1 = jnp.where(tpos >= 1, a1, p1_ref[0, :, cols])
            a2 = jnp.where(tpos >= 2, a2, p2_ref[0, :, cols])
            cs_ref[0, :, cols] = a
        cw = cw_ref[:, cols]
        a_conv = cb_ref[:, cols] + cw[0:1, :] * a2 + cw[1:2, :] * a1 + cw[2:3, :] * a
        gated_sc[:, cols] = (_gelu_tanh(a_conv) * bb).astype(BF16)
    x2 = x1_ref[0] + gate_ref[0] * _dot(gated_sc[...], wd_ref[...])
    if final:
        x2 = _rms(x2) * gf_ref[...]
    xo_ref[0] = x2


def _ffn(h2, x1, gate, w_up, cw, cb, wd, gf, tm, final, prev=None):
    bx, tx, d = x1.shape
    nj = D_FF // 256
    per_row = gate.shape[1] != 1
    mod_spec = pl.BlockSpec((1, tm if per_row else 1, d),
                            (lambda b, i: (b, i, 0)) if per_row else (lambda b, i: (b, 0, 0)))
    row = pl.BlockSpec((1, tm, d), lambda b, i: (b, i, 0))
    const = lambda a: pl.BlockSpec(a.shape, lambda b, i: (0,) * a.ndim, pipeline_mode=pl.Buffered(1))
    in_specs = [row, row, mod_spec, const(w_up), const(cw), const(cb), const(wd), const(gf)]
    args = [h2, x1, gate, w_up, cw, cb, wd, gf]
    scratch = [pltpu.VMEM((tm, D_FF), BF16)]
    if prev is None:
        period = None
        cs_spec = pl.BlockSpec((1, 1, 2, D_FF), lambda b, i: (b, i, 0, 0))
        cs_shape = jax.ShapeDtypeStruct((bx, tx // tm, 2, D_FF), F32)
        scratch.append(pltpu.VMEM((nj, 8, D_FF // nj), F32))
    else:
        period = prev[2]
        in_specs += [pl.BlockSpec((1, tm, D_FF), lambda b, i: (b, i, 0))] * 2
        args += [prev[0], prev[1]]
        cs_spec = pl.BlockSpec((1, tm, D_FF), lambda b, i: (b, i, 0))
        cs_shape = jax.ShapeDtypeStruct((bx, tx, D_FF), F32)
    return pl.pallas_call(
        functools.partial(_ffn_kernel, tm=tm, nj=nj, period=period, final=final),
        grid=(bx, tx // tm),
        in_specs=in_specs,
        out_specs=[row, cs_spec],
        out_shape=[jax.ShapeDtypeStruct((bx, tx, d), F32), cs_shape],
        scratch_shapes=scratch,
        compiler_params=_cparams("arbitrary", "arbitrary"),
        name="ffn",
    )(*args)


def kernel(x_prompt, x_sample, c_prompt, c_sample, cache_k, cache_v, cache_logf, page_table, state_hgrn, state_conv, norm_g, final_norm_g, w_ada, b_ada, w_in, fox_fbias, fox_onorm_g, hgrn_lb_logits, hgrn_onorm_g, w_out, w_up, conv_w, conv_b, w_down):
    depth = w_in.shape[0]
    bp, tp, d = x_prompt.shape
    bs, ts, _ = x_sample.shape
    ns = bs * ts
    nh, hd = N_FOX_HEADS, FOX_HEAD_DIM

    p = jax.nn.softmax(hgrn_lb_logits.astype(F32), axis=0)
    lb_all = jnp.maximum(jnp.cumsum(p, axis=0) - p[0:1], 0.0)

    kt_cache = cache_k.transpose(0, 1, 3, 4, 2)
    vt_cache = cache_v.transpose(0, 1, 3, 4, 2)
    lft_cache = cache_logf.transpose(0, 1, 3, 2)

    mod = _ada(jnp.concatenate([c_prompt, c_sample], axis=0), w_ada, b_ada)
    gf = final_norm_g.reshape(1, d)

    xp = x_prompt
    xs = x_sample.reshape(1, ns, d)
    outs_p = [[] for _ in range(5)]
    outs_s = [[] for _ in range(5)]
    for l in range(depth):
        mp = mod[l, :bp].reshape(bp, 6, 1, d)
        ms = jnp.repeat(mod[l, bp:].reshape(bs, 6, d), ts, axis=0).reshape(1, ns, 6, d)
        ms = jnp.moveaxis(ms, 2, 0)
        mods_p = [mp[:, j] for j in range(6)]
        mods_s = [ms[j] for j in range(6)]

        wl = w_in[l]
        off = 3 * FOX_W + nh
        wn = jnp.concatenate([wl[:, :FOX_W], wl[:, off:]], axis=1).astype(BF16)
        wkv = wl[:, FOX_W:3 * FOX_W].T.astype(BF16)
        wft = wl[:, 3 * FOX_W:off].T.astype(BF16)
        fbt = fox_fbias[l].reshape(nh, 1)
        g1 = norm_g[l, 0].reshape(1, d)
        g2 = norm_g[l, 1].reshape(1, d)
        fog = fox_onorm_g[l].reshape(1, FOX_W)
        hog = hgrn_onorm_g[l].reshape(1, HGRN_W)
        lb = lb_all[l].reshape(1, HGRN_W)
        wo = w_out[l].astype(BF16)
        wu = w_up[l].astype(BF16)
        wd = w_down[l].astype(BF16)
        cw = conv_w[l]
        cb = conv_b[l].reshape(1, D_FF)
        final = l == depth - 1

        (q, kt, vt, ktb, vtb, lft, hq, hf, hi, hg) = _inproj(
            xp, mods_p[1], mods_p[0], g1, wn, wkv, wft, fbt, tm=512)
        fo = _fox_prompt(q, ktb, vtb, _cumsum(lft), fog)
        s0p = jnp.zeros((bp, N_HGRN_HEADS, HGRN_HEAD_DIM, HGRN_HEAD_DIM), F32)
        ho, s_new = _hgrn(hq, hf, hi, hg, lb, hog, s0p, t_valid=tp)
        x1, h2 = _outproj(fo, ho, xp, mods_p[2], mods_p[4], mods_p[3], g2, wo, tm=512)
        xp, conv_p = _ffn(h2, x1, mods_p[5], wu, cw, cb, wd, gf, tm=512, final=final)
        outs_p[0].append(kt.reshape(bp, nh, hd, tp).transpose(0, 3, 1, 2))
        outs_p[1].append(vt.reshape(bp, nh, hd, tp).transpose(0, 3, 1, 2))
        outs_p[2].append(lft.transpose(0, 2, 1))
        outs_p[3].append(s_new)
        outs_p[4].append(conv_p[:, -1])

        (q, kt, vt, ktb, vtb, lft, hq, hf, hi, hg) = _inproj(
            xs, mods_s[1], mods_s[0], g1, wn, wkv, wft, fbt, tm=ns)
        ktn = kt.reshape(FOX_W, bs, ts).transpose(1, 0, 2)
        vtn = vt.reshape(FOX_W, bs, ts).transpose(1, 0, 2)
        k4 = ktn.reshape(bs, nh, hd, ts).transpose(0, 3, 1, 2)
        v4 = vtn.reshape(bs, nh, hd, ts).transpose(0, 3, 1, 2)
        lf3 = lft.reshape(nh, bs, ts).transpose(1, 2, 0)
        q4 = q.reshape(bs, ts, nh, hd)
        eye = jnp.eye(nh, dtype=BF16)
        qbd = (q4[:, :, :, None, :] * eye[None, None, :, :, None]).reshape(bs, ts * nh, FOX_W)
        cum = jnp.cumsum(lf3, axis=1)
        cn = cum.reshape(bs, ts * nh, 1)
        tok = jnp.arange(LANES)
        diff = cum[:, :, None, :] - jnp.pad(cum, ((0, 0), (0, LANES - ts), (0, 0)))[:, None, :, :]
        ok = (tok[None, :] <= jnp.arange(ts)[:, None]) & (tok[None, :] < ts)
        bnew = jnp.where(ok[None, :, :, None], diff, MASK_VALUE)
        bnew = bnew.transpose(0, 1, 3, 2).reshape(bs, ts * nh, LANES)
        pad_t = lambda a: jnp.pad(a, ((0, 0), (0, 0), (0, LANES - ts)))
        fo_s = _decode(page_table, qbd, cn, pad_t(ktn), pad_t(vtn), bnew, fog,
                       kt_cache, vt_cache, lft_cache, l)
        fo_s = fo_s.reshape(1, ns, FOX_W).astype(BF16)

        t_pad = HGRN_CHUNK
        padh = lambda a: jnp.pad(a.reshape(bs, ts, HGRN_W), ((0, 0), (0, t_pad - ts), (0, 0)))
        ho_s, s_new_s = _hgrn(padh(hq), padh(hf), padh(hi), padh(hg), lb, hog, state_hgrn[l],
                              t_valid=ts)
        ho_s = ho_s[:, :ts].reshape(1, ns, HGRN_W)
        x1, h2 = _outproj(fo_s, ho_s, xs, mods_s[2], mods_s[4], mods_s[3], g2, wo, tm=ns)
        buf = state_conv[l]
        zrow = jnp.zeros((bs, 1, D_FF), F32)
        prev1 = jnp.concatenate([buf[:, 1:2], zrow, zrow, zrow], axis=1).reshape(1, ns, D_FF)
        prev2 = jnp.concatenate([buf[:, 0:1], buf[:, 1:2], zrow, zrow], axis=1).reshape(1, ns, D_FF)
        xs, a_full = _ffn(h2, x1, mods_s[5], wu, cw, cb, wd, gf, tm=ns, final=final,
                          prev=(prev1, prev2, ts))
        outs_s[0].append(k4)
        outs_s[1].append(v4)
        outs_s[2].append(lf3)
        outs_s[3].append(s_new_s)
        outs_s[4].append(a_full.reshape(bs, ts, D_FF)[:, ts - 2:])

    return (xp, xs.reshape(bs, ts, d),
            *[jnp.stack(o) for o in outs_p],
            *[jnp.stack(o) for o in outs_s])
```

```python
import functools

import numpy as np
import jax
import jax.numpy as jnp
from jax import lax
from jax.experimental import pallas as pl
from jax.experimental.pallas import tpu as pltpu

F32 = jnp.float32
BF16 = jnp.bfloat16
HIGHEST = lax.Precision.HIGHEST

D_MODEL = 1024
FOX_W = 512
FOX_HEAD_DIM = 64
N_FOX_HEADS = 8
HGRN_W = 512
HGRN_HEAD_DIM = 128
N_HGRN_HEADS = 4
D_FF = 2816
EPS = 1e-6
FOX_SCALE = FOX_HEAD_DIM ** -0.5
MASK_VALUE = -1e30
TINY = 1e-30
NEG_INIT = -1e30
LANES = 128

HGRN_CHUNK = 64
HGRN_ROWS_PER_STEP = 2
HGRN_TIME_BLOCK = 512
PAGES_PER_STEP = 8
DECODE_SLOTS = 4
FOX_BLOCK = 512
VMEM_LIMIT = 56 * 1024 * 1024


def _cparams(*sem):
    return pltpu.CompilerParams(dimension_semantics=sem, vmem_limit_bytes=VMEM_LIMIT)


def _sigmoid(x):
    return 1.0 / (1.0 + jnp.exp(-x))


def _silu(x):
    return x * _sigmoid(x)


def _log_sigmoid(x):
    return jnp.minimum(x, 0.0) - jnp.log(1.0 + jnp.exp(-jnp.abs(x)))


def _gelu_tanh(x):
    c = np.float32(np.sqrt(2.0 / np.pi))
    return 0.5 * x * (1.0 + jnp.tanh(c * (x + 0.044715 * (x * x * x))))


def _rms(x):
    return x * lax.rsqrt(jnp.mean(x * x, axis=-1, keepdims=True) + EPS)


def _dot(a, b):
    return jnp.dot(a, b, preferred_element_type=F32)


def _dot_nt(a, b):
    return lax.dot_general(a, b, (((1,), (1,)), ((), ())), preferred_element_type=F32)


def _dot_tn(a, b):
    return lax.dot_general(a, b, (((0,), (0,)), ((), ())), preferred_element_type=F32)


def _split3(x):
    p0 = x.astype(BF16)
    r1 = x - p0.astype(F32)
    p1 = r1.astype(BF16)
    p2 = (r1 - p1.astype(F32)).astype(BF16)
    return p0, p1, p2


def _ada_kernel(c_ref, w_ref, b_ref, o_ref):
    a = _silu(c_ref[...])
    o_ref[0] = jnp.dot(a, w_ref[0], precision=HIGHEST, preferred_element_type=F32) + b_ref[0]


def _ada(c_all, w_ada, b_ada):
    depth, d, n = w_ada.shape
    nc = c_all.shape[0]
    tn = 1536
    return pl.pallas_call(
        _ada_kernel,
        grid=(depth, n // tn),
        in_specs=[pl.BlockSpec((nc, d), lambda l, j: (0, 0)),
                  pl.BlockSpec((1, d, tn), lambda l, j: (l, 0, j)),
                  pl.BlockSpec((1, 1, tn), lambda l, j: (l, 0, j))],
        out_specs=pl.BlockSpec((1, nc, tn), lambda l, j: (l, 0, j)),
        out_shape=jax.ShapeDtypeStruct((depth, nc, n), F32),
        compiler_params=_cparams("arbitrary", "arbitrary"),
        name="ada",
    )(c_all, w_ada, b_ada.reshape(depth, 1, n))


def _inproj_kernel(x_ref, sc_ref, sh_ref, g_ref, wn_ref, wkv_ref, wft_ref, fbt_ref,
                   q_ref, kt_ref, vt_ref, ktb_ref, vtb_ref, lft_ref,
                   hq_ref, hf_ref, hi_ref, hg_ref):
    h = _rms(x_ref[0]) * g_ref[...]
    h = h * (1.0 + sc_ref[0]) + sh_ref[0]
    hb = h.astype(BF16)

    def grp(i):
        return _dot(hb, wn_ref[:, i * 512:(i + 1) * 512])

    q_ref[0] = (grp(0) * FOX_SCALE).astype(BF16)
    hq_ref[0] = grp(1)
    hf_ref[0] = grp(2)
    hi_ref[0] = grp(3)
    hg_ref[0] = grp(4)
    kvt = _dot_nt(wkv_ref[...], hb)
    kt_ref[0] = kvt[:FOX_W]
    ktb_ref[0] = kvt[:FOX_W].astype(BF16)
    vt_ref[0] = kvt[FOX_W:]
    vtb_ref[0] = kvt[FOX_W:].astype(BF16)
    lft_ref[0] = _log_sigmoid(_dot_nt(wft_ref[...], hb) + fbt_ref[...])


def _inproj(x, sc, sh, g, wn, wkv, wft, fbt, tm):
    bx, tx, d = x.shape
    per_row = sc.shape[1] != 1
    mod_spec = pl.BlockSpec((1, tm if per_row else 1, d),
                            (lambda b, i: (b, i, 0)) if per_row else (lambda b, i: (b, 0, 0)))
    row = lambda c: pl.BlockSpec((1, tm, c), lambda b, i: (b, i, 0))
    col = lambda r: pl.BlockSpec((1, r, tm), lambda b, i: (b, 0, i))
    const = lambda a: pl.BlockSpec(a.shape, lambda b, i: (0,) * a.ndim)
    sds = jax.ShapeDtypeStruct
    outs = [sds((bx, tx, FOX_W), BF16),
            sds((bx, FOX_W, tx), F32), sds((bx, FOX_W, tx), F32),
            sds((bx, FOX_W, tx), BF16), sds((bx, FOX_W, tx), BF16),
            sds((bx, N_FOX_HEADS, tx), F32),
            sds((bx, tx, HGRN_W), F32), sds((bx, tx, HGRN_W), F32), sds((bx, tx, HGRN_W), F32),
            sds((bx, tx, HGRN_W), F32)]
    out_specs = [row(FOX_W)] + [col(FOX_W)] * 4 + [col(N_FOX_HEADS)] + [row(HGRN_W)] * 4
    return pl.pallas_call(
        _inproj_kernel,
        grid=(bx, tx // tm),
        in_specs=[row(d), mod_spec, mod_spec, const(g), const(wn), const(wkv), const(wft),
                  const(fbt)],
        out_specs=out_specs,
        out_shape=outs,
        compiler_params=_cparams("arbitrary", "arbitrary"),
        name="inproj",
    )(x, sc, sh, g, wn, wkv, wft, fbt)


def _cumsum_kernel(lft_ref, triu_ref, crow_ref, *, blk):
    nh, t = lft_ref.shape[1], lft_ref.shape[2]
    carry = jnp.zeros((nh, 1), F32)
    for i in range(t // blk):
        sl = slice(i * blk, (i + 1) * blk)
        r = jnp.dot(lft_ref[0, :, sl], triu_ref[...], precision=HIGHEST,
                    preferred_element_type=F32) + carry
        crow_ref[0, :, sl] = r
        carry = r[:, blk - 1:blk]


def _cumsum(lft):
    b, nh, t = lft.shape
    blk = 256
    triu = jnp.asarray(np.triu(np.ones((blk, blk), np.float32)))
    return pl.pallas_call(
        functools.partial(_cumsum_kernel, blk=blk),
        grid=(b,),
        in_specs=[pl.BlockSpec((1, nh, t), lambda i: (i, 0, 0)),
                  pl.BlockSpec((blk, blk), lambda i: (0, 0))],
        out_specs=pl.BlockSpec((1, nh, t), lambda i: (i, 0, 0)),
        out_shape=jax.ShapeDtypeStruct((b, nh, t), F32),
        compiler_params=_cparams("arbitrary"),
        name="cumsum",
    )(lft, triu)


def _fox_kernel(q_ref, kt_ref, vt_ref, crow_ref, g_ref, o_ref, m_sc, l_sc, acc_sc, *, tb):
    qi = pl.program_id(2)
    q = q_ref[0]
    lane = lax.broadcasted_iota(jnp.int32, (tb, LANES), 1)
    lo = lane < FOX_HEAD_DIM
    qs = (jnp.where(lo, q, jnp.zeros_like(q)), jnp.where(lo, jnp.zeros_like(q), q))
    nc = tb // LANES
    m_sc[...] = jnp.full(m_sc.shape, NEG_INIT, F32)
    l_sc[...] = jnp.zeros(l_sc.shape, F32)
    acc_sc[...] = jnp.zeros(acc_sc.shape, F32)

    def step(kj, masked):
        start = pl.multiple_of(kj * tb, tb)
        kt = kt_ref[0, :, pl.ds(start, tb)]
        vt = vt_ref[0, :, pl.ds(start, tb)]
        parts, m_news, alphas, pbs = [], [], [], []
        for e in range(2):
            ck = crow_ref[0, 0, pl.ds(e, 1), pl.ds(start, tb)]
            s = _dot(qs[e], kt) - ck
            pe = [s[:, c * LANES:(c + 1) * LANES] for c in range(nc)]
            if masked:
                rows = lax.broadcasted_iota(jnp.int32, (tb, LANES), 0)
                cols = lax.broadcasted_iota(jnp.int32, (tb, LANES), 1)
                pe = [jnp.where(rows >= cols + c * LANES, p, MASK_VALUE) for c, p in enumerate(pe)]
            parts.append(pe)
        for e in range(2):
            mt = parts[e][0]
            for p in parts[e][1:]:
                mt = jnp.maximum(mt, p)
            m_prev = m_sc[e]
            m_new = jnp.maximum(m_prev, jnp.max(mt, axis=-1, keepdims=True))
            m_news.append(m_new)
            alphas.append(jnp.exp(m_prev - m_new))
            m_sc[e] = m_new
        for e in range(2):
            ps = [jnp.exp(p - m_news[e]) for p in parts[e]]
            lsum = ps[0]
            for p in ps[1:]:
                lsum = lsum + p
            l_sc[e] = alphas[e] * l_sc[e] + lsum
            pbs.append(jnp.concatenate([p.astype(BF16) for p in ps], axis=1))
        for e in range(2):
            acc_sc[e] = alphas[e] * acc_sc[e] + _dot_nt(pbs[e], vt)

    def body(kj, carry):
        step(kj, False)
        return carry

    lax.fori_loop(0, qi, body, 0)
    step(qi, True)
    o0 = acc_sc[0] / jnp.sum(l_sc[0], axis=-1, keepdims=True)
    o1 = acc_sc[1] / jnp.sum(l_sc[1], axis=-1, keepdims=True)
    o = jnp.where(lo, o0, o1)
    sq = o * o
    s_lo = jnp.sum(jnp.where(lo, sq, 0.0), axis=-1, keepdims=True)
    s_all = jnp.sum(sq, axis=-1, keepdims=True)
    ms = jnp.where(lo, s_lo, s_all - s_lo) * (1.0 / FOX_HEAD_DIM)
    o_ref[0] = (o * lax.rsqrt(ms + EPS) * g_ref[...]).astype(o_ref.dtype)


def _fox_prompt(q, ktb, vtb, crow, g):
    b, t, _ = q.shape
    tb = min(FOX_BLOCK, t)
    nhp = N_FOX_HEADS // 2
    crow4 = crow.reshape(b, nhp, 2, t)
    return pl.pallas_call(
        functools.partial(_fox_kernel, tb=tb),
        grid=(b, nhp, t // tb),
        in_specs=[pl.BlockSpec((1, tb, LANES), lambda i, h, j: (i, j, h)),
                  pl.BlockSpec((1, LANES, t), lambda i, h, j: (i, h, 0)),
                  pl.BlockSpec((1, LANES, t), lambda i, h, j: (i, h, 0)),
                  pl.BlockSpec((1, 1, 2, t), lambda i, h, j: (i, h, 0, 0)),
                  pl.BlockSpec((1, LANES), lambda i, h, j: (0, h))],
        out_specs=pl.BlockSpec((1, tb, LANES), lambda i, h, j: (i, j, h)),
        out_shape=jax.ShapeDtypeStruct((b, t, FOX_W), BF16),
        scratch_shapes=[pltpu.VMEM((2, tb, LANES), F32), pltpu.VMEM((2, tb, LANES), F32),
                        pltpu.VMEM((2, tb, LANES), F32)],
        compiler_params=_cparams("arbitrary", "arbitrary", "arbitrary"),
        name="fox",
    )(q, ktb, vtb, crow4, g)


def _decode_kernel(pt_ref, qbd_ref, cn_ref, knew_ref, vnew_ref, bnew_ref, g_ref, u_ref,
                   kc_ref, vc_ref, lfc_ref, o_ref,
                   kbuf, vbuf, lfbuf, sem, m_sc, l_sc, acc_sc, carry_sc,
                   *, gp, ng, nq, nb, layer):
    g = pl.program_id(1)
    nr = nq * N_FOX_HEADS
    step = pl.program_id(0) * ng + g
    total = nb * ng
    ahead = DECODE_SLOTS - 1

    def page_copies(st, slot):
        row = st // ng
        grp = ng - 1 - st % ng
        out = []
        for i in range(gp):
            page = pt_ref[row, grp * gp + i]
            out.append(pltpu.make_async_copy(kc_ref.at[layer, page], kbuf.at[slot, i], sem.at[slot]))
            out.append(pltpu.make_async_copy(vc_ref.at[layer, page], vbuf.at[slot, i], sem.at[slot]))
            out.append(pltpu.make_async_copy(lfc_ref.at[layer, page], lfbuf.at[slot, i], sem.at[slot]))
        return out

    @pl.when(step == 0)
    def _():
        for st in range(min(ahead, total)):
            for c in page_copies(st, st):
                c.start()

    @pl.when(step + ahead < total)
    def _():
        for c in page_copies(step + ahead, (step + ahead) % DECODE_SLOTS):
            c.start()

    slot = step % DECODE_SLOTS
    for c in page_copies(step, slot):
        c.wait()
    k_refs = [kbuf.at[slot, i] for i in range(gp)]
    v_refs = [vbuf.at[slot, i] for i in range(gp)]
    lf_refs = [lfbuf.at[slot, i] for i in range(gp)]

    @pl.when(g == 0)
    def _():
        m_sc[...] = jnp.full(m_sc.shape, NEG_INIT, F32)
        l_sc[...] = jnp.zeros(l_sc.shape, F32)
        acc_sc[...] = jnp.zeros(acc_sc.shape, F32)
        carry_sc[...] = jnp.zeros(carry_sc.shape, F32)

    qbd = qbd_ref[0]

    def update(s, vt):
        m_prev = m_sc[...]
        m_new = jnp.maximum(m_prev, jnp.max(s, axis=-1, keepdims=True))
        alpha = jnp.exp(m_prev - m_new)
        p = jnp.exp(s - m_new)
        l_sc[...] = alpha * l_sc[...] + jnp.sum(p, axis=-1, keepdims=True)
        acc_sc[...] = alpha * acc_sc[...] + _dot_nt(p.astype(BF16), vt)
        m_sc[...] = m_new

    lf_all = jnp.concatenate([r[...] for r in lf_refs], axis=0)
    r = _dot(jnp.concatenate(_split3(lf_all), axis=0), u_ref[...])
    r = r[:8 * gp] + r[8 * gp:16 * gp] + r[16 * gp:]
    carry = carry_sc[...]
    biases = [None] * gp
    for i in range(gp - 1, -1, -1):
        sl = slice(i * N_FOX_HEADS, (i + 1) * N_FOX_HEADS)
        biases[i] = jnp.concatenate([r[sl, :LANES] + carry] * nq, axis=0)
        carry = carry + r[sl, LANES:]
    carry_sc[...] = carry
    bias = jnp.concatenate(biases, axis=1) + cn_ref[0]
    kt = jnp.concatenate([r[...].reshape(FOX_W, LANES).astype(BF16) for r in k_refs], axis=1)
    vt = jnp.concatenate([r[...].reshape(FOX_W, LANES).astype(BF16) for r in v_refs], axis=1)
    update(_dot(qbd, kt) + bias, vt)

    @pl.when(g == ng - 1)
    def _():
        s_new = _dot(qbd, knew_ref[0].astype(BF16)) + bnew_ref[0]
        s_new = jnp.where(bnew_ref[0] > 0.5 * MASK_VALUE, s_new, MASK_VALUE)
        update(s_new, vnew_ref[0].astype(BF16))
        o = acc_sc[...] / l_sc[...]
        rowh = lax.broadcasted_iota(jnp.int32, (nr, FOX_W), 0) % N_FOX_HEADS
        colh = lax.broadcasted_iota(jnp.int32, (nr, FOX_W), 1) // FOX_HEAD_DIM
        o = jnp.where(rowh == colh, o, 0.0)
        ms = jnp.sum(o * o, axis=-1, keepdims=True) * (1.0 / FOX_HEAD_DIM)
        o = o * lax.rsqrt(ms + EPS) * g_ref[...]
        o_ref[0] = jnp.sum(o.reshape(nq, N_FOX_HEADS, FOX_W), axis=1)


def _decode(page_table, qbd, cn, knew, vnew, bnew, g, kt_cache, vt_cache, lft_cache, layer):
    bs, n_pages = page_table.shape
    nr = qbd.shape[1]
    nq = nr // N_FOX_HEADS
    gp = min(PAGES_PER_STEP, n_pages)
    ng = n_pages // gp
    u = np.zeros((LANES, 2 * LANES), np.float32)
    u[:, :LANES] = np.tril(np.ones((LANES, LANES), np.float32), -1)
    u[:, LANES:] = 1.0
    hbm = pl.BlockSpec(memory_space=pl.ANY)
    in_specs = [pl.BlockSpec((1, nr, FOX_W), lambda b, s, pt: (b, 0, 0)),
                pl.BlockSpec((1, nr, 1), lambda b, s, pt: (b, 0, 0)),
                pl.BlockSpec((1, FOX_W, LANES), lambda b, s, pt: (b, 0, 0)),
                pl.BlockSpec((1, FOX_W, LANES), lambda b, s, pt: (b, 0, 0)),
                pl.BlockSpec((1, nr, LANES), lambda b, s, pt: (b, 0, 0)),
                pl.BlockSpec((1, FOX_W), lambda b, s, pt: (0, 0)),
                pl.BlockSpec((LANES, 2 * LANES), lambda b, s, pt: (0, 0)),
                hbm, hbm, hbm]
    page = (N_FOX_HEADS, FOX_HEAD_DIM, LANES)
    grid_spec = pltpu.PrefetchScalarGridSpec(
        num_scalar_prefetch=1, grid=(bs, ng), in_specs=in_specs,
        out_specs=pl.BlockSpec((1, nq, FOX_W), lambda b, s, pt: (b, 0, 0)),
        scratch_shapes=[pltpu.VMEM((DECODE_SLOTS, gp) + page, F32),
                        pltpu.VMEM((DECODE_SLOTS, gp) + page, F32),
                        pltpu.VMEM((DECODE_SLOTS, gp, N_FOX_HEADS, LANES), F32),
                        pltpu.SemaphoreType.DMA((DECODE_SLOTS,)),
                        pltpu.VMEM((nr, 1), F32), pltpu.VMEM((nr, 1), F32),
                        pltpu.VMEM((nr, FOX_W), F32), pltpu.VMEM((N_FOX_HEADS, LANES), F32)])
    return pl.pallas_call(
        functools.partial(_decode_kernel, gp=gp, ng=ng, nq=nq, nb=bs, layer=layer),
        grid_spec=grid_spec,
        out_shape=jax.ShapeDtypeStruct((bs, nq, FOX_W), F32),
        compiler_params=_cparams("arbitrary", "arbitrary"),
        name="decode",
    )(page_table, qbd, cn, knew, vnew, bnew, g, jnp.asarray(u, BF16),
      kt_cache, vt_cache, lft_cache)


def _hgrn_tables(L):
    nlev = int(np.log2(L))
    rows = []
    t = np.arange(L)[:, None]
    r = np.arange(L)[None, :]
    for n in range(nlev):
        h = L >> (n + 1)
        mid = (t // (2 * h)) * (2 * h) + h
        up = t >= mid
        rows.append(np.where(up, (r >= mid) & (r <= t), (r > t) & (r < mid)))
    rows.append(r <= t)
    rows.append(r > t)
    m_all = np.concatenate(rows, axis=0).astype(np.float32)
    masks = []
    for n in range(nlev):
        h = L >> (n + 1)
        masks.append((t // (2 * h)) == (r // (2 * h)))
    masks.append(t == r)
    return m_all, np.stack(masks).astype(np.float32), nlev


def _hgrn_kernel(hq_ref, hf_ref, hi_ref, hg_ref, lb_ref, g_ref, s0_ref, mall_ref, mask_ref,
                 o_ref, sout_ref, st_sc, *, L, nlev, t_valid):
    nbat, tb = hq_ref.shape[0], hq_ref.shape[1]
    hd = HGRN_HEAD_DIM
    ti = pl.program_id(1)
    chains = [(bi, hh) for bi in range(nbat) for hh in range(N_HGRN_HEADS)]

    @pl.when(ti == 0)
    def _():
        for bi, hh in chains:
            st_sc[bi, hh] = s0_ref[bi, hh].T

    rowi = lax.broadcasted_iota(jnp.int32, (L, hd), 0)

    def chunk(c, carry):
        start = pl.multiple_of(c * L, L)
        rows = pl.ds(start, L)
        qs, kks, vvs, es, accs = [], [], [], [], []
        for bi, hh in chains:
            cols = slice(hh * hd, (hh + 1) * hd)
            lb = lb_ref[:, cols]
            q = hq_ref[bi, rows, cols]
            f = lb + (1.0 - lb) * _sigmoid(hf_ref[bi, rows, cols])
            lf = jnp.log(jnp.maximum(f, TINY))
            kk = 1.0 - f
            if t_valid is not None:
                valid = (rowi + start) < t_valid
                lf = jnp.where(valid, lf, 0.0)
                kk = jnp.where(valid, kk, 0.0)
            p0, p1, _ = _split3(lf)
            res = _dot(mall_ref[...], jnp.concatenate([p0, p1], axis=1))
            qs.append(q)
            kks.append(kk)
            vvs.append(_silu(hi_ref[bi, rows, cols]).astype(BF16))
            es.append(res)
        for i in range(len(chains)):
            es[i] = jnp.exp(es[i][:, :hd] + es[i][:, hd:])
            accs.append(mask_ref[nlev] * _dot_nt(qs[i].astype(BF16), kks[i].astype(BF16)))
        for n in range(nlev):
            up = (rowi & (L >> (n + 1))) != 0
            for i in range(len(chains)):
                en = es[i][n * L:(n + 1) * L]
                qd = jnp.where(up, qs[i] * en, 0.0).astype(BF16)
                kd = jnp.where(up, 0.0, kks[i] * en).astype(BF16)
                accs[i] = accs[i] + mask_ref[n] * _dot_nt(qd, kd)
        outs = []
        for i, (bi, hh) in enumerate(chains):
            eb = es[i][nlev * L:(nlev + 1) * L]
            er = es[i][(nlev + 1) * L:]
            st = st_sc[bi, hh]
            outs.append(_dot(accs[i].astype(BF16), vvs[i])
                        + _dot_nt((qs[i] * eb).astype(BF16), st.astype(BF16)))
            st_sc[bi, hh] = st * eb[L - 1:L, :] + _dot_tn(vvs[i], (kks[i] * er).astype(BF16))
        for i, (bi, hh) in enumerate(chains):
            cols = slice(hh * hd, (hh + 1) * hd)
            o = _rms(outs[i]) * g_ref[:, cols] * _silu(hg_ref[bi, rows, cols])
            o_ref[bi, rows, cols] = o.astype(o_ref.dtype)
        return carry

    lax.fori_loop(0, tb // L, chunk, 0)

    @pl.when(ti == pl.num_programs(1) - 1)
    def _():
        for bi, hh in chains:
            sout_ref[bi, hh] = st_sc[bi, hh].T


def _hgrn(hq, hf, hi, hg, lb, g, s0, t_valid=None):
    b, t_pad, _ = hq.shape
    L = HGRN_CHUNK
    nbat = HGRN_ROWS_PER_STEP if b % HGRN_ROWS_PER_STEP == 0 else 1
    tb = min(HGRN_TIME_BLOCK, t_pad)
    assert t_valid is None or tb == t_pad
    m_all, masks, nlev = _hgrn_tables(L)
    act = pl.BlockSpec((nbat, tb, HGRN_W), lambda i, j: (i, j, 0))
    vec = pl.BlockSpec((1, HGRN_W), lambda i, j: (0, 0))
    st = pl.BlockSpec((nbat, N_HGRN_HEADS, HGRN_HEAD_DIM, HGRN_HEAD_DIM), lambda i, j: (i, 0, 0, 0))
    return pl.pallas_call(
        functools.partial(_hgrn_kernel, L=L, nlev=nlev, t_valid=t_valid),
        grid=(b // nbat, t_pad // tb),
        in_specs=[act, act, act, act, vec, vec, st,
                  pl.BlockSpec(m_all.shape, lambda i, j: (0, 0)),
                  pl.BlockSpec(masks.shape, lambda i, j: (0, 0, 0))],
        out_specs=[act, st],
        out_shape=[jax.ShapeDtypeStruct((b, t_pad, HGRN_W), BF16),
                   jax.ShapeDtypeStruct((b, N_HGRN_HEADS, HGRN_HEAD_DIM, HGRN_HEAD_DIM), F32)],
        scratch_shapes=[pltpu.VMEM((nbat, N_HGRN_HEADS, HGRN_HEAD_DIM, HGRN_HEAD_DIM), F32)],
        compiler_params=_cparams("arbitrary", "arbitrary"),
        name="hgrn",
    )(hq, hf, hi, hg, lb, g, s0, jnp.asarray(m_all, BF16), jnp.asarray(masks))


def _outproj_kernel(fo_ref, ho_ref, x_ref, gate_ref, sc_ref, sh_ref, g_ref, w_ref, x1_ref, h2_ref):
    mixed = _dot(fo_ref[0], w_ref[:FOX_W, :]) + _dot(ho_ref[0], w_ref[FOX_W:, :])
    x1 = x_ref[0] + gate_ref[0] * mixed
    x1_ref[0] = x1
    h2 = _rms(x1) * g_ref[...]
    h2_ref[0] = (h2 * (1.0 + sc_ref[0]) + sh_ref[0]).astype(BF16)


def _outproj(fo, ho, x, gate, sc, sh, g, w, tm):
    bx, tx, d = x.shape
    per_row = gate.shape[1] != 1
    mod_spec = pl.BlockSpec((1, tm if per_row else 1, d),
                            (lambda b, i: (b, i, 0)) if per_row else (lambda b, i: (b, 0, 0)))
    row = lambda c: pl.BlockSpec((1, tm, c), lambda b, i: (b, i, 0))
    return pl.pallas_call(
        _outproj_kernel,
        grid=(bx, tx // tm),
        in_specs=[row(FOX_W), row(HGRN_W), row(d), mod_spec, mod_spec, mod_spec,
                  pl.BlockSpec((1, d), lambda b, i: (0, 0)),
                  pl.BlockSpec(w.shape, lambda b, i: (0, 0))],
        out_specs=[row(d), row(d)],
        out_shape=[jax.ShapeDtypeStruct((bx, tx, d), F32), jax.ShapeDtypeStruct((bx, tx, d), BF16)],
        compiler_params=_cparams("arbitrary", "arbitrary"),
        name="outproj",
    )(fo, ho, x, gate, sc, sh, g, w)


def _ffn_kernel(*refs, tm, nj, period, final):
    if period is None:
        (h2_ref, x1_ref, gate_ref, w_ref, cw_ref, cb_ref, wd_ref, gf_ref,
         xo_ref, cs_ref, gated_sc, halo_sc) = refs
    else:
        (h2_ref, x1_ref, gate_ref, w_ref, cw_ref, cb_ref, wd_ref, gf_ref, p1_ref, p2_ref,
         xo_ref, cs_ref, gated_sc) = refs
    i = pl.program_id(1)
    h2 = h2_ref[0]
    tn = D_FF // nj
    rowi = lax.broadcasted_iota(jnp.int32, (tm, tn), 0)
    if period is None:
        @pl.when(i == 0)
        def _():
            halo_sc[...] = jnp.zeros(halo_sc.shape, F32)

    for j in range(nj):
        cols = slice(j * tn, (j + 1) * tn)
        a = _dot(h2, w_ref[:, cols])
        bb = _dot(h2, w_ref[:, D_FF + j * tn:D_FF + (j + 1) * tn])
        a1 = pltpu.roll(a, 1, axis=0)
        a2 = pltpu.roll(a, 2, axis=0)
        if period is None:
            hl = halo_sc[j]
            a1 = jnp.where(rowi == 0, hl[7:8, :], a1)
            a2 = jnp.where(rowi == 0, hl[6:7, :], jnp.where(rowi == 1, hl[7:8, :], a2))
            halo_sc[j] = a[tm - 8:, :]
            cs_ref[0, 0, :, cols] = a[tm - 2:, :]
        else:
            tpos = rowi % period
            a1 = jnp.where(tpos >= 1, a1, p1_ref[0, :, cols])
            a2 = jnp.where(tpos >= 2, a2, p2_ref[0, :, cols])
            cs_ref[0, :, cols] = a
        cw = cw_ref[:, cols]
        a_conv = cb_ref[:, cols] + cw[0:1, :] * a2 + cw[1:2, :] * a1 + cw[2:3, :] * a
        gated_sc[:, cols] = (_gelu_tanh(a_conv) * bb).astype(BF16)
    x2 = x1_ref[0] + gate_ref[0] * _dot(gated_sc[...], wd_ref[...])
    if final:
        x2 = _rms(x2) * gf_ref[...]
    xo_ref[0] = x2


def _ffn(h2, x1, gate, w_up, cw, cb, wd, gf, tm, final, prev=None):
    bx, tx, d = x1.shape
    nj = D_FF // 256
    per_row = gate.shape[1] != 1
    mod_spec = pl.BlockSpec((1, tm if per_row else 1, d),
                            (lambda b, i: (b, i, 0)) if per_row else (lambda b, i: (b, 0, 0)))
    row = pl.BlockSpec((1, tm, d), lambda b, i: (b, i, 0))
    const = lambda a: pl.BlockSpec(a.shape, lambda b, i: (0,) * a.ndim, pipeline_mode=pl.Buffered(1))
    in_specs = [row, row, mod_spec, const(w_up), const(cw), const(cb), const(wd), const(gf)]
    args = [h2, x1, gate, w_up, cw, cb, wd, gf]
    scratch = [pltpu.VMEM((tm, D_FF), BF16)]
    if prev is None:
        period = None
        cs_spec = pl.BlockSpec((1, 1, 2, D_FF), lambda b, i: (b, i, 0, 0))
        cs_shape = jax.ShapeDtypeStruct((bx, tx // tm, 2, D_FF), F32)
        scratch.append(pltpu.VMEM((nj, 8, D_FF // nj), F32))
    else:
        period = prev[2]
        in_specs += [pl.BlockSpec((1, tm, D_FF), lambda b, i: (b, i, 0))] * 2
        args += [prev[0], prev[1]]
        cs_spec = pl.BlockSpec((1, tm, D_FF), lambda b, i: (b, i, 0))
        cs_shape = jax.ShapeDtypeStruct((bx, tx, D_FF), F32)
    return pl.pallas_call(
        functools.partial(_ffn_kernel, tm=tm, nj=nj, period=period, final=final),
        grid=(bx, tx // tm),
        in_specs=in_specs,
        out_specs=[row, cs_spec],
        out_shape=[jax.ShapeDtypeStruct((bx, tx, d), F32), cs_shape],
        scratch_shapes=scratch,
        compiler_params=_cparams("arbitrary", "arbitrary"),
        name="ffn",
    )(*args)


def kernel(x_prompt, x_sample, c_prompt, c_sample, cache_k, cache_v, cache_logf, page_table, state_hgrn, state_conv, norm_g, final_norm_g, w_ada, b_ada, w_in, fox_fbias, fox_onorm_g, hgrn_lb_logits, hgrn_onorm_g, w_out, w_up, conv_w, conv_b, w_down):
    depth = w_in.shape[0]
    bp, tp, d = x_prompt.shape
    bs, ts, _ = x_sample.shape
    ns = bs * ts
    nh, hd = N_FOX_HEADS, FOX_HEAD_DIM

    p = jax.nn.softmax(hgrn_lb_logits.astype(F32), axis=0)
    lb_all = jnp.maximum(jnp.cumsum(p, axis=0) - p[0:1], 0.0)

    kt_cache = cache_k.transpose(0, 1, 3, 4, 2)
    vt_cache = cache_v.transpose(0, 1, 3, 4, 2)
    lft_cache = cache_logf.transpose(0, 1, 3, 2)

    mod = _ada(jnp.concatenate([c_prompt, c_sample], axis=0), w_ada, b_ada)
    gf = final_norm_g.reshape(1, d)

    xp = x_prompt
    xs = x_sample.reshape(1, ns, d)
    outs_p = [[] for _ in range(5)]
    outs_s = [[] for _ in range(5)]
    for l in range(depth):
        mp = mod[l, :bp].reshape(bp, 6, 1, d)
        ms = jnp.repeat(mod[l, bp:].reshape(bs, 6, d), ts, axis=0).reshape(1, ns, 6, d)
        ms = jnp.moveaxis(ms, 2, 0)
        mods_p = [mp[:, j] for j in range(6)]
        mods_s = [ms[j] for j in range(6)]

        wl = w_in[l]
        off = 3 * FOX_W + nh
        wn = jnp.concatenate([wl[:, :FOX_W], wl[:, off:]], axis=1).astype(BF16)
        wkv = wl[:, FOX_W:3 * FOX_W].T.astype(BF16)
        wft = wl[:, 3 * FOX_W:off].T.astype(BF16)
        fbt = fox_fbias[l].reshape(nh, 1)
        g1 = norm_g[l, 0].reshape(1, d)
        g2 = norm_g[l, 1].reshape(1, d)
        fog = fox_onorm_g[l].reshape(1, FOX_W)
        hog = hgrn_onorm_g[l].reshape(1, HGRN_W)
        lb = lb_all[l].reshape(1, HGRN_W)
        wo = w_out[l].astype(BF16)
        wu = w_up[l].astype(BF16)
        wd = w_down[l].astype(BF16)
        cw = conv_w[l]
        cb = conv_b[l].reshape(1, D_FF)
        final = l == depth - 1

        (q, kt, vt, ktb, vtb, lft, hq, hf, hi, hg) = _inproj(
            xp, mods_p[1], mods_p[0], g1, wn, wkv, wft, fbt, tm=512)
        fo = _fox_prompt(q, ktb, vtb, _cumsum(lft), fog)
        s0p = jnp.zeros((bp, N_HGRN_HEADS, HGRN_HEAD_DIM, HGRN_HEAD_DIM), F32)
        ho, s_new = _hgrn(hq, hf, hi, hg, lb, hog, s0p)
        x1, h2 = _outproj(fo, ho, xp, mods_p[2], mods_p[4], mods_p[3], g2, wo, tm=512)
        xp, conv_p = _ffn(h2, x1, mods_p[5], wu, cw, cb, wd, gf, tm=512, final=final)
        outs_p[0].append(kt.reshape(bp, nh, hd, tp).transpose(0, 3, 1, 2))
        outs_p[1].append(vt.reshape(bp, nh, hd, tp).transpose(0, 3, 1, 2))
        outs_p[2].append(lft.transpose(0, 2, 1))
        outs_p[3].append(s_new)
        outs_p[4].append(conv_p[:, -1])

        (q, kt, vt, ktb, vtb, lft, hq, hf, hi, hg) = _inproj(
            xs, mods_s[1], mods_s[0], g1, wn, wkv, wft, fbt, tm=ns)
        ktn = kt.reshape(FOX_W, bs, ts).transpose(1, 0, 2)
        vtn = vt.reshape(FOX_W, bs, ts).transpose(1, 0, 2)
        k4 = ktn.reshape(bs, nh, hd, ts).transpose(0, 3, 1, 2)
        v4 = vtn.reshape(bs, nh, hd, ts).transpose(0, 3, 1, 2)
        lf3 = lft.reshape(nh, bs, ts).transpose(1, 2, 0)
        q4 = q.reshape(bs, ts, nh, hd)
        eye = jnp.eye(nh, dtype=BF16)
        qbd = (q4[:, :, :, None, :] * eye[None, None, :, :, None]).reshape(bs, ts * nh, FOX_W)
        cum = jnp.cumsum(lf3, axis=1)
        cn = cum.reshape(bs, ts * nh, 1)
        tok = jnp.arange(LANES)
        diff = cum[:, :, None, :] - jnp.pad(cum, ((0, 0), (0, LANES - ts), (0, 0)))[:, None, :, :]
        ok = (tok[None, :] <= jnp.arange(ts)[:, None]) & (tok[None, :] < ts)
        bnew = jnp.where(ok[None, :, :, None], diff, MASK_VALUE)
        bnew = bnew.transpose(0, 1, 3, 2).reshape(bs, ts * nh, LANES)
        pad_t = lambda a: jnp.pad(a, ((0, 0), (0, 0), (0, LANES - ts)))
        fo_s = _decode(page_table, qbd, cn, pad_t(ktn), pad_t(vtn), bnew, fog,
                       kt_cache, vt_cache, lft_cache, l)
        fo_s = fo_s.reshape(1, ns, FOX_W).astype(BF16)

        t_pad = HGRN_CHUNK
        padh = lambda a: jnp.pad(a.reshape(bs, ts, HGRN_W), ((0, 0), (0, t_pad - ts), (0, 0)))
        ho_s, s_new_s = _hgrn(padh(hq), padh(hf), padh(hi), padh(hg), lb, hog, state_hgrn[l],
                              t_valid=ts)
        ho_s = ho_s[:, :ts].reshape(1, ns, HGRN_W)
        x1, h2 = _outproj(fo_s, ho_s, xs, mods_s[2], mods_s[4], mods_s[3], g2, wo, tm=ns)
        buf = state_conv[l]
        zrow = jnp.zeros((bs, 1, D_FF), F32)
        prev1 = jnp.concatenate([buf[:, 1:2], zrow, zrow, zrow], axis=1).reshape(1, ns, D_FF)
        prev2 = jnp.concatenate([buf[:, 0:1], buf[:, 1:2], zrow, zrow], axis=1).reshape(1, ns, D_FF)
        xs, a_full = _ffn(h2, x1, mods_s[5], wu, cw, cb, wd, gf, tm=ns, final=final,
                          prev=(prev1, prev2, ts))
        outs_s[0].append(k4)
        outs_s[1].append(v4)
        outs_s[2].append(lf3)
        outs_s[3].append(s_new_s)
        outs_s[4].append(a_full.reshape(bs, ts, D_FF)[:, ts - 2:])

    return (xp, xs.reshape(bs, ts, d),
            *[jnp.stack(o) for o in outs_p],
            *[jnp.stack(o) for o in outs_s])
```

```python
import functools

import numpy as np
import jax
import jax.numpy as jnp
from jax import lax
from jax.experimental import pallas as pl
from jax.experimental.pallas import tpu as pltpu

F32 = jnp.float32
BF16 = jnp.bfloat16
HIGHEST = lax.Precision.HIGHEST

D_MODEL = 1024
FOX_W = 512
FOX_HEAD_DIM = 64
N_FOX_HEADS = 8
HGRN_W = 512
HGRN_HEAD_DIM = 128
N_HGRN_HEADS = 4
D_FF = 2816
EPS = 1e-6
FOX_SCALE = FOX_HEAD_DIM ** -0.5
LOG2E = float(np.log2(np.e))
MASK_VALUE = -1e30
TINY = 1e-30
NEG_INIT = -1e30
LANES = 128

HGRN_CHUNK = 64
HGRN_ROWS_PER_STEP = 2
HGRN_TIME_BLOCK = 512
PAGES_PER_STEP = 8
DECODE_SLOTS = 4
FOX_BLOCK = 512
VMEM_LIMIT = 56 * 1024 * 1024


def _cparams(*sem):
    return pltpu.CompilerParams(dimension_semantics=sem, vmem_limit_bytes=VMEM_LIMIT)


def _sigmoid(x):
    return 1.0 / (1.0 + jnp.exp(-x))


def _silu(x):
    return x * _sigmoid(x)


def _log_sigmoid(x):
    return jnp.minimum(x, 0.0) - jnp.log(1.0 + jnp.exp(-jnp.abs(x)))


def _gelu_tanh(x):
    c = np.float32(np.sqrt(2.0 / np.pi))
    return 0.5 * x * (1.0 + jnp.tanh(c * (x + 0.044715 * (x * x * x))))


def _rms(x):
    return x * lax.rsqrt(jnp.mean(x * x, axis=-1, keepdims=True) + EPS)


def _dot(a, b):
    return jnp.dot(a, b, preferred_element_type=F32)


def _dot_nt(a, b):
    return lax.dot_general(a, b, (((1,), (1,)), ((), ())), preferred_element_type=F32)


def _dot_tn(a, b):
    return lax.dot_general(a, b, (((0,), (0,)), ((), ())), preferred_element_type=F32)


def _running_sum(x, axis):
    parts = [lax.index_in_dim(x, 0, axis, keepdims=True)]
    for i in range(1, x.shape[axis]):
        parts.append(parts[-1] + lax.index_in_dim(x, i, axis, keepdims=True))
    return jnp.concatenate(parts, axis=axis)


def _split3(x):
    p0 = x.astype(BF16)
    r1 = x - p0.astype(F32)
    p1 = r1.astype(BF16)
    p2 = (r1 - p1.astype(F32)).astype(BF16)
    return p0, p1, p2


def _ada_kernel(c_ref, w_ref, b_ref, o_ref):
    a = _silu(c_ref[...])
    o_ref[0] = jnp.dot(a, w_ref[0], precision=HIGHEST, preferred_element_type=F32) + b_ref[0]


def _ada(c_all, w_ada, b_ada):
    depth, d, n = w_ada.shape
    nc = c_all.shape[0]
    tn = 1536
    return pl.pallas_call(
        _ada_kernel,
        grid=(depth, n // tn),
        in_specs=[pl.BlockSpec((nc, d), lambda l, j: (0, 0)),
                  pl.BlockSpec((1, d, tn), lambda l, j: (l, 0, j)),
                  pl.BlockSpec((1, 1, tn), lambda l, j: (l, 0, j))],
        out_specs=pl.BlockSpec((1, nc, tn), lambda l, j: (l, 0, j)),
        out_shape=jax.ShapeDtypeStruct((depth, nc, n), F32),
        compiler_params=_cparams("arbitrary", "arbitrary"),
        name="ada",
    )(c_all, w_ada, b_ada.reshape(depth, 1, n))


def _inproj_kernel(*refs, has_prev, q_scale):
    (x_ref, sc_ref, sh_ref, g_ref, wn_ref, wkv_ref, wft_ref, fbt_ref) = refs[:8]
    (q_ref, kt_ref, vt_ref, ktb_ref, vtb_ref, lft_ref,
     hq_ref, hf_ref, hi_ref, hg_ref) = refs[8 + (2 if has_prev else 0):]
    h = _rms(x_ref[0]) * g_ref[...]
    h = h * (1.0 + sc_ref[0]) + sh_ref[0]
    hb = h.astype(BF16)

    def grp(i):
        return _dot(hb, wn_ref[0, :, i * 512:(i + 1) * 512])

    q_ref[0] = (grp(0) * q_scale).astype(BF16)
    hq_ref[0] = grp(1)
    hf_ref[0] = grp(2)
    hi_ref[0] = grp(3)
    hg_ref[0] = grp(4)
    kvt = _dot_nt(wkv_ref[0], hb)
    for dl in range(kt_ref.shape[0]):
        kt_ref[dl, 0] = kvt[:FOX_W]
        vt_ref[dl, 0] = kvt[FOX_W:]
    ktb_ref[0] = kvt[:FOX_W].astype(BF16)
    vtb_ref[0] = kvt[FOX_W:].astype(BF16)
    lft_ref[0] = _log_sigmoid(_dot_nt(wft_ref[0], hb) + fbt_ref[...])


def _inproj(x, sc, sh, g, wn, wkv, wft, fbt, tm, layer, q_scale, kv_prev=None):
    bx, tx, d = x.shape
    depth = wn.shape[0]
    per_row = sc.shape[1] != 1
    mod_spec = pl.BlockSpec((1, tm if per_row else 1, d),
                            (lambda b, i: (b, i, 0)) if per_row else (lambda b, i: (b, 0, 0)))
    row = lambda c: pl.BlockSpec((1, tm, c), lambda b, i: (b, i, 0))
    col = lambda r: pl.BlockSpec((1, r, tm), lambda b, i: (b, 0, i))
    const = lambda a: pl.BlockSpec(a.shape, lambda b, i: (0,) * a.ndim)
    lay = lambda a: pl.BlockSpec((1,) + a.shape[1:], lambda b, i: (layer, 0, 0))
    if kv_prev is None:
        assert layer == 0
        slab = pl.BlockSpec((depth, 1, FOX_W, tm), lambda b, i: (0, b, 0, i))
    else:
        slab = pl.BlockSpec((1, 1, FOX_W, tm), lambda b, i: (layer, b, 0, i))
    sds = jax.ShapeDtypeStruct
    outs = [sds((bx, tx, FOX_W), BF16),
            sds((depth, bx, FOX_W, tx), F32), sds((depth, bx, FOX_W, tx), F32),
            sds((bx, FOX_W, tx), BF16), sds((bx, FOX_W, tx), BF16),
            sds((bx, N_FOX_HEADS, tx), F32),
            sds((bx, tx, HGRN_W), F32), sds((bx, tx, HGRN_W), F32), sds((bx, tx, HGRN_W), F32),
            sds((bx, tx, HGRN_W), F32)]
    out_specs = ([row(FOX_W), slab, slab] + [col(FOX_W)] * 2 + [col(N_FOX_HEADS)]
                 + [row(HGRN_W)] * 4)
    in_specs = [row(d), mod_spec, mod_spec, const(g), lay(wn), lay(wkv), lay(wft), const(fbt)]
    args = [x, sc, sh, g, wn, wkv, wft, fbt]
    aliases = {}
    if kv_prev is not None:
        in_specs += [pl.BlockSpec(memory_space=pl.ANY)] * 2
        args += list(kv_prev)
        aliases = {8: 1, 9: 2}
    return pl.pallas_call(
        functools.partial(_inproj_kernel, has_prev=kv_prev is not None, q_scale=q_scale),
        grid=(bx, tx // tm),
        in_specs=in_specs,
        out_specs=out_specs,
        out_shape=outs,
        input_output_aliases=aliases,
        compiler_params=_cparams("arbitrary", "arbitrary"),
        name="inproj",
    )(*args)


def _cumsum_kernel(lft_ref, triu_ref, crow_ref, *, blk):
    nh, t = lft_ref.shape[1], lft_ref.shape[2]
    carry = jnp.zeros((nh, 1), F32)
    for i in range(t // blk):
        sl = slice(i * blk, (i + 1) * blk)
        r = jnp.dot(lft_ref[0, :, sl], triu_ref[...], precision=HIGHEST,
                    preferred_element_type=F32) + carry
        crow_ref[0, :, sl] = r * LOG2E
        carry = r[:, blk - 1:blk]


def _cumsum(lft):
    b, nh, t = lft.shape
    blk = 256
    triu = jnp.asarray(np.triu(np.ones((blk, blk), np.float32)))
    return pl.pallas_call(
        functools.partial(_cumsum_kernel, blk=blk),
        grid=(b,),
        in_specs=[pl.BlockSpec((1, nh, t), lambda i: (i, 0, 0)),
                  pl.BlockSpec((blk, blk), lambda i: (0, 0))],
        out_specs=pl.BlockSpec((1, nh, t), lambda i: (i, 0, 0)),
        out_shape=jax.ShapeDtypeStruct((b, nh, t), F32),
        compiler_params=_cparams("arbitrary"),
        name="cumsum",
    )(lft, triu)


def _fox_kernel(q_ref, kt_ref, vt_ref, crow_ref, g_ref, o_ref, m_sc, l_sc, acc_sc, *, tb):
    qi = pl.program_id(2)
    q = q_ref[0]
    lane = lax.broadcasted_iota(jnp.int32, (tb, LANES), 1)
    lo = lane < FOX_HEAD_DIM
    qs = (jnp.where(lo, q, jnp.zeros_like(q)), jnp.where(lo, jnp.zeros_like(q), q))
    nc = tb // LANES
    m_sc[...] = jnp.full(m_sc.shape, NEG_INIT, F32)
    l_sc[...] = jnp.zeros(l_sc.shape, F32)
    acc_sc[...] = jnp.zeros(acc_sc.shape, F32)

    def step(kj, masked):
        start = pl.multiple_of(kj * tb, tb)
        kt = kt_ref[0, :, pl.ds(start, tb)]
        vt = vt_ref[0, :, pl.ds(start, tb)]
        parts, m_news, alphas, pbs = [], [], [], []
        for e in range(2):
            ck = crow_ref[0, 0, pl.ds(e, 1), pl.ds(start, tb)]
            s = _dot(qs[e], kt) - ck
            pe = [s[:, c * LANES:(c + 1) * LANES] for c in range(nc)]
            if masked:
                rows = lax.broadcasted_iota(jnp.int32, (tb, LANES), 0)
                cols = lax.broadcasted_iota(jnp.int32, (tb, LANES), 1)
                pe = [jnp.where(rows >= cols + c * LANES, p, MASK_VALUE) for c, p in enumerate(pe)]
            parts.append(pe)
        for e in range(2):
            mt = parts[e][0]
            for p in parts[e][1:]:
                mt = jnp.maximum(mt, p)
            m_prev = m_sc[e]
            m_new = jnp.maximum(m_prev, jnp.max(mt, axis=-1, keepdims=True))
            m_news.append(m_new)
            alphas.append(jnp.exp2(m_prev - m_new))
            m_sc[e] = m_new
        for e in range(2):
            ps = [jnp.exp2(p - m_news[e]) for p in parts[e]]
            lsum = ps[0]
            for p in ps[1:]:
                lsum = lsum + p
            l_sc[e] = alphas[e] * l_sc[e] + lsum
            pbs.append(jnp.concatenate([p.astype(BF16) for p in ps], axis=1))
        for e in range(2):
            acc_sc[e] = alphas[e] * acc_sc[e] + _dot_nt(pbs[e], vt)

    def body(kj, carry):
        step(kj, False)
        return carry

    lax.fori_loop(0, qi, body, 0)
    step(qi, True)
    o0 = acc_sc[0] / jnp.sum(l_sc[0], axis=-1, keepdims=True)
    o1 = acc_sc[1] / jnp.sum(l_sc[1], axis=-1, keepdims=True)
    o = jnp.where(lo, o0, o1)
    sq = o * o
    s_lo = jnp.sum(jnp.where(lo, sq, 0.0), axis=-1, keepdims=True)
    s_all = jnp.sum(sq, axis=-1, keepdims=True)
    ms = jnp.where(lo, s_lo, s_all - s_lo) * (1.0 / FOX_HEAD_DIM)
    o_ref[0] = (o * lax.rsqrt(ms + EPS) * g_ref[...]).astype(o_ref.dtype)


def _fox_prompt(q, ktb, vtb, crow, g):
    b, t, _ = q.shape
    tb = min(FOX_BLOCK, t)
    nhp = N_FOX_HEADS // 2
    crow4 = crow.reshape(b, nhp, 2, t)
    return pl.pallas_call(
        functools.partial(_fox_kernel, tb=tb),
        grid=(b, nhp, t // tb),
        in_specs=[pl.BlockSpec((1, tb, LANES), lambda i, h, j: (i, j, h)),
                  pl.BlockSpec((1, LANES, t), lambda i, h, j: (i, h, 0)),
                  pl.BlockSpec((1, LANES, t), lambda i, h, j: (i, h, 0)),
                  pl.BlockSpec((1, 1, 2, t), lambda i, h, j: (i, h, 0, 0)),
                  pl.BlockSpec((1, LANES), lambda i, h, j: (0, h))],
        out_specs=pl.BlockSpec((1, tb, LANES), lambda i, h, j: (i, j, h)),
        out_shape=jax.ShapeDtypeStruct((b, t, FOX_W), BF16),
        scratch_shapes=[pltpu.VMEM((2, tb, LANES), F32), pltpu.VMEM((2, tb, LANES), F32),
                        pltpu.VMEM((2, tb, LANES), F32)],
        compiler_params=_cparams("arbitrary", "arbitrary", "arbitrary"),
        name="fox",
    )(q, ktb, vtb, crow4, g)


def _decode_step(step, row_in, write_out, pt_ref, knew_ref, vnew_ref, g_ref, u_ref,
                 kc_ref, vc_ref, lfc_ref, kbuf, vbuf, lfbuf, sem, m_sc, l_sc, acc_sc, carry_sc,
                 *, gp, ng, nq, nb, layer):
    g = step % ng
    nr = nq * N_FOX_HEADS
    total = nb * ng
    ahead = DECODE_SLOTS - 1
    qbd, cn, bnew = row_in(step // ng)

    def page_copies(st, slot):
        row = st // ng
        grp = ng - 1 - st % ng
        out = []
        for i in range(gp):
            page = pt_ref[row, grp * gp + i]
            out.append(pltpu.make_async_copy(kc_ref.at[layer, page], kbuf.at[slot, i], sem.at[slot]))
            out.append(pltpu.make_async_copy(vc_ref.at[layer, page], vbuf.at[slot, i], sem.at[slot]))
            out.append(pltpu.make_async_copy(lfc_ref.at[layer, page], lfbuf.at[slot, i], sem.at[slot]))
        return out

    @pl.when(step == 0)
    def _():
        for st in range(min(ahead, total)):
            for c in page_copies(st, st):
                c.start()

    @pl.when(step + ahead < total)
    def _():
        for c in page_copies(step + ahead, (step + ahead) % DECODE_SLOTS):
            c.start()

    slot = step % DECODE_SLOTS
    for c in page_copies(step, slot):
        c.wait()
    k_refs = [kbuf.at[slot, i] for i in range(gp)]
    v_refs = [vbuf.at[slot, i] for i in range(gp)]
    lf_refs = [lfbuf.at[slot, i] for i in range(gp)]

    @pl.when(g == 0)
    def _():
        m_sc[...] = jnp.full(m_sc.shape, NEG_INIT, F32)
        l_sc[...] = jnp.zeros(l_sc.shape, F32)
        acc_sc[...] = jnp.zeros(acc_sc.shape, F32)
        carry_sc[...] = jnp.zeros(carry_sc.shape, F32)

    def update(s, vt):
        m_prev = m_sc[...]
        m_new = jnp.maximum(m_prev, jnp.max(s, axis=-1, keepdims=True))
        alpha = jnp.exp(m_prev - m_new)
        p = jnp.exp(s - m_new)
        l_sc[...] = alpha * l_sc[...] + jnp.sum(p, axis=-1, keepdims=True)
        acc_sc[...] = alpha * acc_sc[...] + _dot_nt(p.astype(BF16), vt)
        m_sc[...] = m_new

    lf_all = jnp.concatenate([r[...] for r in lf_refs], axis=0)
    r = _dot(jnp.concatenate(_split3(lf_all), axis=0), u_ref[...])
    r = r[:8 * gp] + r[8 * gp:16 * gp] + r[16 * gp:]
    carry = carry_sc[...]
    biases = [None] * gp
    for i in range(gp - 1, -1, -1):
        sl = slice(i * N_FOX_HEADS, (i + 1) * N_FOX_HEADS)
        biases[i] = jnp.concatenate([r[sl, :LANES] + carry] * nq, axis=0)
        carry = carry + r[sl, LANES:]
    carry_sc[...] = carry
    bias = jnp.concatenate(biases, axis=1) + cn
    kt = jnp.concatenate([r[...].reshape(FOX_W, LANES).astype(BF16) for r in k_refs], axis=1)
    vt = jnp.concatenate([r[...].reshape(FOX_W, LANES).astype(BF16) for r in v_refs], axis=1)
    update(_dot(qbd, kt) + bias, vt)

    @pl.when(g == ng - 1)
    def _():
        s_new = _dot(qbd, knew_ref[0].astype(BF16)) + bnew
        s_new = jnp.where(bnew > 0.5 * MASK_VALUE, s_new, MASK_VALUE)
        update(s_new, vnew_ref[0].astype(BF16))
        o = acc_sc[...] / l_sc[...]
        rowh = lax.broadcasted_iota(jnp.int32, (nr, FOX_W), 0) % N_FOX_HEADS
        colh = lax.broadcasted_iota(jnp.int32, (nr, FOX_W), 1) // FOX_HEAD_DIM
        o = jnp.where(rowh == colh, o, 0.0)
        ms = jnp.sum(o * o, axis=-1, keepdims=True) * (1.0 / FOX_HEAD_DIM)
        o = o * lax.rsqrt(ms + EPS) * g_ref[...]
        write_out(step // ng, jnp.sum(o.reshape(nq, N_FOX_HEADS, FOX_W), axis=1))


def _decode_scratch(gp, nr):
    page = (N_FOX_HEADS, FOX_HEAD_DIM, LANES)
    return [pltpu.VMEM((DECODE_SLOTS, gp) + page, F32),
            pltpu.VMEM((DECODE_SLOTS, gp) + page, F32),
            pltpu.VMEM((DECODE_SLOTS, gp, N_FOX_HEADS, LANES), F32),
            pltpu.SemaphoreType.DMA((DECODE_SLOTS,)),
            pltpu.VMEM((nr, 1), F32), pltpu.VMEM((nr, 1), F32),
            pltpu.VMEM((nr, FOX_W), F32), pltpu.VMEM((N_FOX_HEADS, LANES), F32)]


def _suffix_matrix():
    u = np.zeros((LANES, 2 * LANES), np.float32)
    u[:, :LANES] = np.tril(np.ones((LANES, LANES), np.float32), -1)
    u[:, LANES:] = 1.0
    return jnp.asarray(u, BF16)


def _decode_kernel(pt_ref, qbd_ref, cn_ref, knew_ref, vnew_ref, bnew_ref, g_ref, u_ref,
                   kc_ref, vc_ref, lfc_ref, o_ref, *scratch, ng, **statics):
    def write_out(row, val):
        o_ref[0] = val

    _decode_step(pl.program_id(0) * ng + pl.program_id(1),
                 lambda row: (qbd_ref[0], cn_ref[0], bnew_ref[0]), write_out,
                 pt_ref, knew_ref, vnew_ref, g_ref, u_ref, kc_ref, vc_ref, lfc_ref, *scratch,
                 ng=ng, **statics)


def _decode(page_table, qbd, cn, knew, vnew, bnew, g, kt_cache, vt_cache, lft_cache, layer):
    bs, n_pages = page_table.shape
    nr = qbd.shape[1]
    nq = nr // N_FOX_HEADS
    gp = min(PAGES_PER_STEP, n_pages)
    ng = n_pages // gp
    hbm = pl.BlockSpec(memory_space=pl.ANY)
    in_specs = [pl.BlockSpec((1, nr, FOX_W), lambda b, s, pt: (b, 0, 0)),
                pl.BlockSpec((1, nr, 1), lambda b, s, pt: (b, 0, 0)),
                pl.BlockSpec(knew.shape, lambda b, s, pt: (0, 0, 0)),
                pl.BlockSpec(vnew.shape, lambda b, s, pt: (0, 0, 0)),
                pl.BlockSpec((1, nr, bnew.shape[2]), lambda b, s, pt: (b, 0, 0)),
                pl.BlockSpec((1, FOX_W), lambda b, s, pt: (0, 0)),
                pl.BlockSpec((LANES, 2 * LANES), lambda b, s, pt: (0, 0)),
                hbm, hbm, hbm]
    grid_spec = pltpu.PrefetchScalarGridSpec(
        num_scalar_prefetch=1, grid=(bs, ng), in_specs=in_specs,
        out_specs=pl.BlockSpec((1, nq, FOX_W), lambda b, s, pt: (b, 0, 0)),
        scratch_shapes=_decode_scratch(gp, nr))
    return pl.pallas_call(
        functools.partial(_decode_kernel, gp=gp, ng=ng, nq=nq, nb=bs, layer=layer),
        grid_spec=grid_spec,
        out_shape=jax.ShapeDtypeStruct((bs, nq, FOX_W), F32),
        compiler_params=_cparams("arbitrary", "arbitrary"),
        name="decode",
    )(page_table, qbd, cn, knew, vnew, bnew, g, _suffix_matrix(), kt_cache, vt_cache, lft_cache)


def _hgrn_tables(L):
    nlev = int(np.log2(L))
    rows = []
    t = np.arange(L)[:, None]
    r = np.arange(L)[None, :]
    for n in range(nlev):
        h = L >> (n + 1)
        mid = (t // (2 * h)) * (2 * h) + h
        up = t >= mid
        rows.append(np.where(up, (r >= mid) & (r <= t), (r > t) & (r < mid)))
    rows.append(r <= t)
    rows.append(r > t)
    m_all = np.concatenate(rows, axis=0).astype(np.float32)
    masks = []
    for n in range(nlev):
        h = L >> (n + 1)
        masks.append((t // (2 * h)) == (r // (2 * h)))
    masks.append(t == r)
    return m_all, np.stack(masks).astype(np.float32), nlev


def _hgrn_kernel(*refs, L, nlev, t_valid, dec):
    if dec is None:
        (hq_ref, hf_ref, hi_ref, hg_ref, lb_ref, g_ref, s0_ref, mall_ref, mask_ref,
         o_ref, sout_ref, st_sc) = refs
    else:
        pt_ref = refs[0]
        (hq_ref, hf_ref, hi_ref, hg_ref, lb_ref, g_ref, s0_ref, mall_ref, mask_ref) = refs[1:10]
        (qbd_ref, cn_ref, knew_ref, vnew_ref, bnew_ref, fg_ref, u_ref,
         kc_ref, vc_ref, lfc_ref) = refs[10:20]
        o_ref, sout_ref, fo_ref, st_sc = refs[20:24]
        dec_scratch = refs[24:]
        dec = dict(dec)
        per_chunk = dec.pop("per_chunk")

        def write_out(row, val):
            fo_ref[row] = val
    nbat, tb = hq_ref.shape[0], hq_ref.shape[1]
    hd = HGRN_HEAD_DIM
    ti = pl.program_id(1)
    chains = [(bi, hh) for bi in range(nbat) for hh in range(N_HGRN_HEADS)]

    @pl.when(ti == 0)
    def _():
        for bi, hh in chains:
            st_sc[bi, hh] = s0_ref[bi, hh].T

    rowi = lax.broadcasted_iota(jnp.int32, (L, hd), 0)

    def chunk(c, carry):
        start = pl.multiple_of(c * L, L)
        rows = pl.ds(start, L)
        qs, kks, vvs, es, accs = [], [], [], [], []
        for bi, hh in chains:
            cols = slice(hh * hd, (hh + 1) * hd)
            lb = lb_ref[:, cols]
            q = hq_ref[bi, rows, cols]
            f = lb + (1.0 - lb) * _sigmoid(hf_ref[bi, rows, cols])
            lf = jnp.log(jnp.maximum(f, TINY))
            kk = 1.0 - f
            if t_valid is not None:
                valid = (rowi + start) < t_valid
                lf = jnp.where(valid, lf, 0.0)
                kk = jnp.where(valid, kk, 0.0)
            p0, p1, _ = _split3(lf)
            res = _dot(mall_ref[...], jnp.concatenate([p0, p1], axis=1))
            qs.append(q)
            kks.append(kk)
            vvs.append(_silu(hi_ref[bi, rows, cols]).astype(BF16))
            es.append(res)
        for i in range(len(chains)):
            es[i] = jnp.exp(es[i][:, :hd] + es[i][:, hd:])
            accs.append(mask_ref[nlev] * _dot_nt(qs[i].astype(BF16), kks[i].astype(BF16)))
        for n in range(nlev):
            up = (rowi & (L >> (n + 1))) != 0
            for i in range(len(chains)):
                en = es[i][n * L:(n + 1) * L]
                qd = jnp.where(up, qs[i] * en, 0.0).astype(BF16)
                kd = jnp.where(up, 0.0, kks[i] * en).astype(BF16)
                accs[i] = accs[i] + mask_ref[n] * _dot_nt(qd, kd)
        outs = []
        for i, (bi, hh) in enumerate(chains):
            eb = es[i][nlev * L:(nlev + 1) * L]
            er = es[i][(nlev + 1) * L:]
            st = st_sc[bi, hh]
            outs.append(_dot(accs[i].astype(BF16), vvs[i])
                        + _dot_nt((qs[i] * eb).astype(BF16), st.astype(BF16)))
            st_sc[bi, hh] = st * eb[L - 1:L, :] + _dot_tn(vvs[i], (kks[i] * er).astype(BF16))
        for i, (bi, hh) in enumerate(chains):
            cols = slice(hh * hd, (hh + 1) * hd)
            o = _rms(outs[i]) * g_ref[:, cols] * _silu(hg_ref[bi, rows, cols])
            o_ref[bi, rows, cols] = o.astype(o_ref.dtype)
        if dec is not None:
            it = (pl.program_id(0) * pl.num_programs(1) + ti) * (tb // L) + c
            for k in range(per_chunk):
                _decode_step(it * per_chunk + k,
                             lambda row: (qbd_ref[row], cn_ref[row], bnew_ref[row]), write_out,
                             pt_ref, knew_ref, vnew_ref, fg_ref, u_ref, kc_ref, vc_ref, lfc_ref,
                             *dec_scratch, **dec)
        return carry

    lax.fori_loop(0, tb // L, chunk, 0)

    @pl.when(ti == pl.num_programs(1) - 1)
    def _():
        for bi, hh in chains:
            sout_ref[bi, hh] = st_sc[bi, hh].T


def _hgrn_grid(b, t_pad):
    nbat = HGRN_ROWS_PER_STEP if b % HGRN_ROWS_PER_STEP == 0 else 1
    tb = min(HGRN_TIME_BLOCK, t_pad)
    return nbat, tb, (b // nbat) * (t_pad // tb) * (tb // HGRN_CHUNK)


def _hgrn(hq, hf, hi, hg, lb, g, s0, t_valid=None, decode_args=None):
    b, t_pad, _ = hq.shape
    L = HGRN_CHUNK
    nbat, tb, n_chunks = _hgrn_grid(b, t_pad)
    assert t_valid is None or tb == t_pad
    m_all, masks, nlev = _hgrn_tables(L)
    im = (lambda f: f) if decode_args is None else (lambda f: (lambda i, j, pt: f(i, j)))
    act = pl.BlockSpec((nbat, tb, HGRN_W), im(lambda i, j: (i, j, 0)))
    vec = pl.BlockSpec((1, HGRN_W), im(lambda i, j: (0, 0)))
    st = pl.BlockSpec((nbat, N_HGRN_HEADS, HGRN_HEAD_DIM, HGRN_HEAD_DIM),
                      im(lambda i, j: (i, 0, 0, 0)))
    const = lambda a: pl.BlockSpec(a.shape, im(lambda i, j: (0,) * a.ndim))
    mall_b, masks_j = jnp.asarray(m_all, BF16), jnp.asarray(masks)
    in_specs = [act, act, act, act, vec, vec, st, const(mall_b), const(masks_j)]
    args = [hq, hf, hi, hg, lb, g, s0, mall_b, masks_j]
    out_specs = [act, st]
    out_shape = [jax.ShapeDtypeStruct((b, t_pad, HGRN_W), BF16),
                 jax.ShapeDtypeStruct((b, N_HGRN_HEADS, HGRN_HEAD_DIM, HGRN_HEAD_DIM), F32)]
    scratch = [pltpu.VMEM((nbat, N_HGRN_HEADS, HGRN_HEAD_DIM, HGRN_HEAD_DIM), F32)]
    grid = (b // nbat, t_pad // tb)
    if decode_args is None:
        return pl.pallas_call(
            functools.partial(_hgrn_kernel, L=L, nlev=nlev, t_valid=t_valid, dec=None),
            grid=grid, in_specs=in_specs, out_specs=out_specs, out_shape=out_shape,
            scratch_shapes=scratch,
            compiler_params=_cparams("arbitrary", "arbitrary"),
            name="hgrn",
        )(*args)
    (page_table, qbd, cn, knew, vnew, bnew, fg, kt_cache, vt_cache, lft_cache, layer) = decode_args
    bs, n_pages = page_table.shape
    nr = qbd.shape[1]
    nq = nr // N_FOX_HEADS
    gp = min(PAGES_PER_STEP, n_pages)
    ng = n_pages // gp
    hbm = pl.BlockSpec(memory_space=pl.ANY)
    u = _suffix_matrix()
    in_specs += [const(qbd), const(cn), const(knew), const(vnew), const(bnew), const(fg), const(u),
                 hbm, hbm, hbm]
    args += [qbd, cn, knew, vnew, bnew, fg, u, kt_cache, vt_cache, lft_cache]
    out_specs.append(pl.BlockSpec((bs, nq, FOX_W), im(lambda i, j: (0, 0, 0))))
    out_shape.append(jax.ShapeDtypeStruct((bs, nq, FOX_W), F32))
    dec = dict(gp=gp, ng=ng, nq=nq, nb=bs, layer=layer, per_chunk=(bs * ng) // n_chunks)
    grid_spec = pltpu.PrefetchScalarGridSpec(
        num_scalar_prefetch=1, grid=grid, in_specs=in_specs, out_specs=out_specs,
        scratch_shapes=scratch + _decode_scratch(gp, nr))
    return pl.pallas_call(
        functools.partial(_hgrn_kernel, L=L, nlev=nlev, t_valid=t_valid, dec=dec),
        grid_spec=grid_spec, out_shape=out_shape,
        compiler_params=_cparams("arbitrary", "arbitrary"),
        name="hgrn_decode",
    )(page_table, *args)


def _decode_fits_hgrn(page_table, b, t):
    bs, n_pages = page_table.shape
    steps = bs * (n_pages // min(PAGES_PER_STEP, n_pages))
    n_chunks = _hgrn_grid(b, t)[2]
    return steps % n_chunks == 0


def _outproj_kernel(fo_ref, ho_ref, x_ref, gate_ref, sc_ref, sh_ref, g_ref, w_ref, x1_ref, h2_ref):
    mixed = _dot(fo_ref[0], w_ref[0, :FOX_W, :]) + _dot(ho_ref[0], w_ref[0, FOX_W:, :])
    x1 = x_ref[0] + gate_ref[0] * mixed
    x1_ref[0] = x1
    h2 = _rms(x1) * g_ref[...]
    h2_ref[0] = (h2 * (1.0 + sc_ref[0]) + sh_ref[0]).astype(BF16)


def _outproj(fo, ho, x, gate, sc, sh, g, w, tm, layer):
    bx, tx, d = x.shape
    per_row = gate.shape[1] != 1
    mod_spec = pl.BlockSpec((1, tm if per_row else 1, d),
                            (lambda b, i: (b, i, 0)) if per_row else (lambda b, i: (b, 0, 0)))
    row = lambda c: pl.BlockSpec((1, tm, c), lambda b, i: (b, i, 0))
    return pl.pallas_call(
        _outproj_kernel,
        grid=(bx, tx // tm),
        in_specs=[row(FOX_W), row(HGRN_W), row(d), mod_spec, mod_spec, mod_spec,
                  pl.BlockSpec((1, d), lambda b, i: (0, 0)),
                  pl.BlockSpec((1,) + w.shape[1:], lambda b, i: (layer, 0, 0))],
        out_specs=[row(d), row(d)],
        out_shape=[jax.ShapeDtypeStruct((bx, tx, d), F32), jax.ShapeDtypeStruct((bx, tx, d), BF16)],
        compiler_params=_cparams("arbitrary", "arbitrary"),
        name="outproj",
    )(fo, ho, x, gate, sc, sh, g, w)


def _ffn_kernel(*refs, tm, nj, period, final):
    if period is None:
        (h2_ref, x1_ref, gate_ref, w_ref, cw_ref, cb_ref, wd_ref, gf_ref,
         xo_ref, cs_ref, gated_sc, halo_sc) = refs
    else:
        (h2_ref, x1_ref, gate_ref, w_ref, cw_ref, cb_ref, wd_ref, gf_ref, p1_ref, p2_ref,
         xo_ref, cs_ref, gated_sc) = refs
    i = pl.program_id(1)
    h2 = h2_ref[0]
    tn = D_FF // nj
    rowi = lax.broadcasted_iota(jnp.int32, (tm, tn), 0)
    if period is None:
        @pl.when(i == 0)
        def _():
            halo_sc[...] = jnp.zeros(halo_sc.shape, F32)

    for j in range(nj):
        cols = slice(j * tn, (j + 1) * tn)
        a = _dot(h2, w_ref[0, :, cols])
        bb = _dot(h2, w_ref[0, :, D_FF + j * tn:D_FF + (j + 1) * tn])
        a1 = pltpu.roll(a, 1, axis=0)
        a2 = pltpu.roll(a, 2, axis=0)
        if period is None:
            hl = halo_sc[j]
            a1 = jnp.where(rowi == 0, hl[7:8, :], a1)
            a2 = jnp.where(rowi == 0, hl[6:7, :], jnp.where(rowi == 1, hl[7:8, :], a2))
            halo_sc[j] = a[tm - 8:, :]
            cs_ref[0, 0, :, cols] = a[tm - 2:, :]
        else:
            tpos = rowi % period
            a1 = jnp.where(tpos >= 1, a1, p1_ref[0, :, cols])
            a2 = jnp.where(tpos >= 2, a2, p2_ref[0, :, cols])
            cs_ref[0, :, cols] = a
        cw = cw_ref[:, cols]
        a_conv = cb_ref[:, cols] + cw[0:1, :] * a2 + cw[1:2, :] * a1 + cw[2:3, :] * a
        gated_sc[:, cols] = (_gelu_tanh(a_conv) * bb).astype(BF16)
    x2 = x1_ref[0] + gate_ref[0] * _dot(gated_sc[...], wd_ref[0])
    if final:
        x2 = _rms(x2) * gf_ref[...]
    xo_ref[0] = x2


def _ffn(h2, x1, gate, w_up, cw, cb, wd, gf, tm, final, layer, prev=None):
    bx, tx, d = x1.shape
    nj = D_FF // 256
    per_row = gate.shape[1] != 1
    mod_spec = pl.BlockSpec((1, tm if per_row else 1, d),
                            (lambda b, i: (b, i, 0)) if per_row else (lambda b, i: (b, 0, 0)))
    row = pl.BlockSpec((1, tm, d), lambda b, i: (b, i, 0))
    const = lambda a: pl.BlockSpec(a.shape, lambda b, i: (0,) * a.ndim, pipeline_mode=pl.Buffered(1))
    lay = lambda a: pl.BlockSpec((1,) + a.shape[1:], lambda b, i: (layer, 0, 0),
                                 pipeline_mode=pl.Buffered(1))
    in_specs = [row, row, mod_spec, lay(w_up), const(cw), const(cb), lay(wd), const(gf)]
    args = [h2, x1, gate, w_up, cw, cb, wd, gf]
    scratch = [pltpu.VMEM((tm, D_FF), BF16)]
    if prev is None:
        period = None
        cs_spec = pl.BlockSpec((1, 1, 2, D_FF), lambda b, i: (b, i, 0, 0))
        cs_shape = jax.ShapeDtypeStruct((bx, tx // tm, 2, D_FF), F32)
        scratch.append(pltpu.VMEM((nj, 8, D_FF // nj), F32))
    else:
        period = prev[2]
        in_specs += [pl.BlockSpec((1, tm, D_FF), lambda b, i: (b, i, 0))] * 2
        args += [prev[0], prev[1]]
        cs_spec = pl.BlockSpec((1, tm, D_FF), lambda b, i: (b, i, 0))
        cs_shape = jax.ShapeDtypeStruct((bx, tx, D_FF), F32)
    return pl.pallas_call(
        functools.partial(_ffn_kernel, tm=tm, nj=nj, period=period, final=final),
        grid=(bx, tx // tm),
        in_specs=in_specs,
        out_specs=[row, cs_spec],
        out_shape=[jax.ShapeDtypeStruct((bx, tx, d), F32), cs_shape],
        scratch_shapes=scratch,
        compiler_params=_cparams("arbitrary", "arbitrary"),
        name="ffn",
    )(*args)


def kernel(x_prompt, x_sample, c_prompt, c_sample, cache_k, cache_v, cache_logf, page_table, state_hgrn, state_conv, norm_g, final_norm_g, w_ada, b_ada, w_in, fox_fbias, fox_onorm_g, hgrn_lb_logits, hgrn_onorm_g, w_out, w_up, conv_w, conv_b, w_down):
    depth = w_in.shape[0]
    bp, tp, d = x_prompt.shape
    bs, ts, _ = x_sample.shape
    ns = bs * ts
    nh, hd = N_FOX_HEADS, FOX_HEAD_DIM

    p = jax.nn.softmax(hgrn_lb_logits.astype(F32), axis=0)
    lb_all = jnp.maximum(_running_sum(p, axis=0) - p[0:1], 0.0)

    kt_cache = cache_k.transpose(0, 1, 3, 4, 2)
    vt_cache = cache_v.transpose(0, 1, 3, 4, 2)
    lft_cache = cache_logf.transpose(0, 1, 3, 2)

    mod = _ada(jnp.concatenate([c_prompt, c_sample], axis=0), w_ada, b_ada)
    gf = final_norm_g.reshape(1, d)

    off = 3 * FOX_W + nh
    wn = jnp.concatenate([w_in[:, :, :FOX_W], w_in[:, :, off:]], axis=2).astype(BF16)
    wkv = w_in[:, :, FOX_W:3 * FOX_W].transpose(0, 2, 1).astype(BF16)
    wft = w_in[:, :, 3 * FOX_W:off].transpose(0, 2, 1).astype(BF16)
    wo = w_out.astype(BF16)
    wu = w_up.astype(BF16)
    wd = w_down.astype(BF16)

    xp = x_prompt
    xs = x_sample.reshape(1, ns, d)
    outs_p = [[] for _ in range(5)]
    outs_s = [[] for _ in range(5)]
    kv_p = kv_s = None
    for l in range(depth):
        mp = mod[l, :bp].reshape(bp, 6, 1, d)
        ms = jnp.repeat(mod[l, bp:].reshape(bs, 6, d), ts, axis=0).reshape(1, ns, 6, d)
        ms = jnp.moveaxis(ms, 2, 0)
        mods_p = [mp[:, j] for j in range(6)]
        mods_s = [ms[j] for j in range(6)]

        fbt = fox_fbias[l].reshape(nh, 1)
        g1 = norm_g[l, 0].reshape(1, d)
        g2 = norm_g[l, 1].reshape(1, d)
        fog = fox_onorm_g[l].reshape(1, FOX_W)
        hog = hgrn_onorm_g[l].reshape(1, HGRN_W)
        lb = lb_all[l].reshape(1, HGRN_W)
        cw = conv_w[l]
        cb = conv_b[l].reshape(1, D_FF)
        final = l == depth - 1

        (q_s, kt_s, vt_s, _, _, lft_s, hq_s, hf_s, hi_s, hg_s) = _inproj(
            xs, mods_s[1], mods_s[0], g1, wn, wkv, wft, fbt, tm=ns, layer=l,
            q_scale=FOX_SCALE, kv_prev=kv_s)
        kv_s = (kt_s, vt_s)
        lf3 = lft_s.reshape(nh, bs, ts).transpose(1, 2, 0)
        q4 = q_s.reshape(bs, ts, nh, hd)
        eye = jnp.eye(nh, dtype=BF16)
        qbd = (q4[:, :, :, None, :] * eye[None, None, :, :, None]).reshape(bs, ts * nh, FOX_W)
        cum = _running_sum(lf3, axis=1)
        cn = cum.reshape(bs, ts * nh, 1)
        col = jnp.arange(ns)
        ok = ((col[None, None, :] // ts == jnp.arange(bs)[:, None, None])
              & (col[None, None, :] % ts <= jnp.arange(ts)[None, :, None]))
        diff = cum[:, :, :, None] - cum.reshape(ns, nh).T[None, None, :, :]
        bnew = jnp.where(ok[:, :, None, :], diff, MASK_VALUE).reshape(bs, ts * nh, ns)
        decode_args = (page_table, qbd, cn, kt_s[l], vt_s[l], bnew, fog,
                       kt_cache, vt_cache, lft_cache, l)

        (q, kt, vt, ktb, vtb, lft, hq, hf, hi, hg) = _inproj(
            xp, mods_p[1], mods_p[0], g1, wn, wkv, wft, fbt, tm=512, layer=l,
            q_scale=FOX_SCALE * LOG2E, kv_prev=kv_p)
        kv_p = (kt, vt)
        fo = _fox_prompt(q, ktb, vtb, _cumsum(lft), fog)
        s0p = jnp.zeros((bp, N_HGRN_HEADS, HGRN_HEAD_DIM, HGRN_HEAD_DIM), F32)
        if _decode_fits_hgrn(page_table, bp, tp):
            ho, s_new, fo_s = _hgrn(hq, hf, hi, hg, lb, hog, s0p, decode_args=decode_args)
        else:
            ho, s_new = _hgrn(hq, hf, hi, hg, lb, hog, s0p)
            fo_s = _decode(*decode_args)
        x1, h2 = _outproj(fo, ho, xp, mods_p[2], mods_p[4], mods_p[3], g2, wo, tm=512, layer=l)
        xp, conv_p = _ffn(h2, x1, mods_p[5], wu, cw, cb, wd, gf, tm=512, final=final, layer=l)
        outs_p[2].append(lft.transpose(0, 2, 1))
        outs_p[3].append(s_new)
        outs_p[4].append(conv_p[:, -1])

        fo_s = fo_s.reshape(1, ns, FOX_W).astype(BF16)
        t_pad = HGRN_CHUNK
        padh = lambda a: jnp.pad(a.reshape(bs, ts, HGRN_W), ((0, 0), (0, t_pad - ts), (0, 0)))
        ho_s, s_new_s = _hgrn(padh(hq_s), padh(hf_s), padh(hi_s), padh(hg_s), lb, hog,
                              state_hgrn[l], t_valid=ts)
        ho_s = ho_s[:, :ts].reshape(1, ns, HGRN_W)
        x1, h2 = _outproj(fo_s, ho_s, xs, mods_s[2], mods_s[4], mods_s[3], g2, wo, tm=ns, layer=l)
        buf = state_conv[l]
        zrow = jnp.zeros((bs, 1, D_FF), F32)
        prev1 = jnp.concatenate([buf[:, 1:2], zrow, zrow, zrow], axis=1).reshape(1, ns, D_FF)
        prev2 = jnp.concatenate([buf[:, 0:1], buf[:, 1:2], zrow, zrow], axis=1).reshape(1, ns, D_FF)
        xs, a_full = _ffn(h2, x1, mods_s[5], wu, cw, cb, wd, gf, tm=ns, final=final, layer=l,
                          prev=(prev1, prev2, ts))
        outs_s[2].append(lf3)
        outs_s[3].append(s_new_s)
        outs_s[4].append(a_full.reshape(bs, ts, D_FF)[:, ts - 2:])

    kv_prompt = [a.reshape(depth, bp, nh, hd, tp).transpose(0, 1, 4, 2, 3) for a in kv_p]
    kv_sample = [a[:, 0].transpose(0, 2, 1).reshape(depth, bs, ts, nh, hd) for a in kv_s]
    return (xp, xs.reshape(bs, ts, d),
            *kv_prompt, *[jnp.stack(o) for o in outs_p[2:]],
            *kv_sample, *[jnp.stack(o) for o in outs_s[2:]])
```

```python
import functools

import numpy as np
import jax
import jax.numpy as jnp
from jax import lax
from jax.experimental import pallas as pl
from jax.experimental.pallas import tpu as pltpu

F32 = jnp.float32
BF16 = jnp.bfloat16
HIGHEST = lax.Precision.HIGHEST

D_MODEL = 1024
FOX_W = 512
FOX_HEAD_DIM = 64
N_FOX_HEADS = 8
HGRN_W = 512
HGRN_HEAD_DIM = 128
N_HGRN_HEADS = 4
D_FF = 2816
EPS = 1e-6
FOX_SCALE = FOX_HEAD_DIM ** -0.5
LOG2E = float(np.log2(np.e))
MASK_VALUE = -1e30
TINY = 1e-30
NEG_INIT = -1e30
LANES = 128

HGRN_CHUNK = 64
HGRN_ROWS_PER_STEP = 2
HGRN_TIME_BLOCK = 512
PAGES_PER_STEP = 8
DECODE_SLOTS = 4
FOX_BLOCK = 512
VMEM_LIMIT = 56 * 1024 * 1024


def _cparams(*sem):
    return pltpu.CompilerParams(dimension_semantics=sem, vmem_limit_bytes=VMEM_LIMIT)


def _sigmoid(x):
    return 1.0 / (1.0 + jnp.exp(-x))


def _silu(x):
    return x * _sigmoid(x)


def _log_sigmoid(x):
    return jnp.minimum(x, 0.0) - jnp.log(1.0 + jnp.exp(-jnp.abs(x)))


def _gelu_tanh(x):
    c = np.float32(np.sqrt(2.0 / np.pi))
    return 0.5 * x * (1.0 + jnp.tanh(c * (x + 0.044715 * (x * x * x))))


def _rms(x):
    return x * lax.rsqrt(jnp.mean(x * x, axis=-1, keepdims=True) + EPS)


def _dot(a, b):
    return jnp.dot(a, b, preferred_element_type=F32)


def _dot_nt(a, b):
    return lax.dot_general(a, b, (((1,), (1,)), ((), ())), preferred_element_type=F32)


def _dot_tn(a, b):
    return lax.dot_general(a, b, (((0,), (0,)), ((), ())), preferred_element_type=F32)


def _running_sum(x, axis):
    parts = [lax.index_in_dim(x, 0, axis, keepdims=True)]
    for i in range(1, x.shape[axis]):
        parts.append(parts[-1] + lax.index_in_dim(x, i, axis, keepdims=True))
    return jnp.concatenate(parts, axis=axis)


def _split3(x):
    p0 = x.astype(BF16)
    r1 = x - p0.astype(F32)
    p1 = r1.astype(BF16)
    p2 = (r1 - p1.astype(F32)).astype(BF16)
    return p0, p1, p2


def _ada_kernel(c_ref, w_ref, b_ref, o_ref):
    a = _silu(c_ref[...])
    nc = a.shape[0]
    a0 = a.astype(BF16).astype(F32)
    lhs = jnp.concatenate([a0, a - a0], axis=0).astype(BF16)
    w = w_ref[0]
    w0 = w.astype(BF16)
    w1 = (w - w0.astype(F32)).astype(BF16)
    r = _dot(lhs, w0)
    o_ref[0] = r[:nc] + r[nc:] + _dot(a.astype(BF16), w1) + b_ref[0]


def _ada(c_all, w_ada, b_ada):
    depth, d, n = w_ada.shape
    nc = c_all.shape[0]
    tn = 1536
    return pl.pallas_call(
        _ada_kernel,
        grid=(depth, n // tn),
        in_specs=[pl.BlockSpec((nc, d), lambda l, j: (0, 0)),
                  pl.BlockSpec((1, d, tn), lambda l, j: (l, 0, j)),
                  pl.BlockSpec((1, 1, tn), lambda l, j: (l, 0, j))],
        out_specs=pl.BlockSpec((1, nc, tn), lambda l, j: (l, 0, j)),
        out_shape=jax.ShapeDtypeStruct((depth, nc, n), F32),
        compiler_params=_cparams("arbitrary", "arbitrary"),
        name="ada",
    )(c_all, w_ada, b_ada.reshape(depth, 1, n))


def _inproj_kernel(*refs, has_prev, q_scale):
    (x_ref, sc_ref, sh_ref, g_ref, wn_ref, wkv_ref, wft_ref, fbt_ref) = refs[:8]
    (q_ref, kt_ref, vt_ref, ktb_ref, vtb_ref, lft_ref,
     hq_ref, hf_ref, hi_ref, hg_ref) = refs[8 + (2 if has_prev else 0):]
    h = _rms(x_ref[0]) * g_ref[...]
    h = h * (1.0 + sc_ref[0]) + sh_ref[0]
    hb = h.astype(BF16)

    def grp(i):
        return _dot(hb, wn_ref[0, :, i * 512:(i + 1) * 512])

    q_ref[0] = (grp(0) * q_scale).astype(BF16)
    hq_ref[0] = grp(1)
    hf_ref[0] = grp(2)
    hi_ref[0] = grp(3)
    hg_ref[0] = grp(4)
    kvt = _dot_nt(wkv_ref[0], hb)
    for dl in range(kt_ref.shape[0]):
        kt_ref[dl, 0] = kvt[:FOX_W]
        vt_ref[dl, 0] = kvt[FOX_W:]
    ktb_ref[0] = kvt[:FOX_W].astype(BF16)
    vtb_ref[0] = kvt[FOX_W:].astype(BF16)
    lft_ref[0] = _log_sigmoid(_dot_nt(wft_ref[0], hb) + fbt_ref[...])


def _inproj(x, sc, sh, g, wn, wkv, wft, fbt, tm, layer, q_scale, kv_prev=None):
    bx, tx, d = x.shape
    depth = wn.shape[0]
    per_row = sc.shape[1] != 1
    mod_spec = pl.BlockSpec((1, tm if per_row else 1, d),
                            (lambda b, i: (b, i, 0)) if per_row else (lambda b, i: (b, 0, 0)))
    row = lambda c: pl.BlockSpec((1, tm, c), lambda b, i: (b, i, 0))
    col = lambda r: pl.BlockSpec((1, r, tm), lambda b, i: (b, 0, i))
    const = lambda a: pl.BlockSpec(a.shape, lambda b, i: (0,) * a.ndim)
    lay = lambda a: pl.BlockSpec((1,) + a.shape[1:], lambda b, i: (layer, 0, 0))
    if kv_prev is None:
        assert layer == 0
        slab = pl.BlockSpec((depth, 1, FOX_W, tm), lambda b, i: (0, b, 0, i))
    else:
        slab = pl.BlockSpec((1, 1, FOX_W, tm), lambda b, i: (layer, b, 0, i))
    sds = jax.ShapeDtypeStruct
    outs = [sds((bx, tx, FOX_W), BF16),
            sds((depth, bx, FOX_W, tx), F32), sds((depth, bx, FOX_W, tx), F32),
            sds((bx, FOX_W, tx), BF16), sds((bx, FOX_W, tx), BF16),
            sds((bx, N_FOX_HEADS, tx), F32),
            sds((bx, tx, HGRN_W), F32), sds((bx, tx, HGRN_W), F32), sds((bx, tx, HGRN_W), F32),
            sds((bx, tx, HGRN_W), F32)]
    out_specs = ([row(FOX_W), slab, slab] + [col(FOX_W)] * 2 + [col(N_FOX_HEADS)]
                 + [row(HGRN_W)] * 4)
    in_specs = [row(d), mod_spec, mod_spec, const(g), lay(wn), lay(wkv), lay(wft), const(fbt)]
    args = [x, sc, sh, g, wn, wkv, wft, fbt]
    aliases = {}
    if kv_prev is not None:
        in_specs += [pl.BlockSpec(memory_space=pl.ANY)] * 2
        args += list(kv_prev)
        aliases = {8: 1, 9: 2}
    return pl.pallas_call(
        functools.partial(_inproj_kernel, has_prev=kv_prev is not None, q_scale=q_scale),
        grid=(bx, tx // tm),
        in_specs=in_specs,
        out_specs=out_specs,
        out_shape=outs,
        input_output_aliases=aliases,
        compiler_params=_cparams("arbitrary", "arbitrary"),
        name="inproj",
    )(*args)


def _cumsum_kernel(lft_ref, triu_ref, crow_ref, *, blk):
    nh, t = lft_ref.shape[1], lft_ref.shape[2]
    carry = jnp.zeros((nh, 1), F32)
    for i in range(t // blk):
        sl = slice(i * blk, (i + 1) * blk)
        r = jnp.dot(lft_ref[0, :, sl], triu_ref[...], precision=HIGHEST,
                    preferred_element_type=F32) + carry
        crow_ref[0, :, sl] = r * LOG2E
        carry = r[:, blk - 1:blk]


def _cumsum(lft):
    b, nh, t = lft.shape
    blk = 256
    triu = jnp.asarray(np.triu(np.ones((blk, blk), np.float32)))
    return pl.pallas_call(
        functools.partial(_cumsum_kernel, blk=blk),
        grid=(b,),
        in_specs=[pl.BlockSpec((1, nh, t), lambda i: (i, 0, 0)),
                  pl.BlockSpec((blk, blk), lambda i: (0, 0))],
        out_specs=pl.BlockSpec((1, nh, t), lambda i: (i, 0, 0)),
        out_shape=jax.ShapeDtypeStruct((b, nh, t), F32),
        compiler_params=_cparams("arbitrary"),
        name="cumsum",
    )(lft, triu)


def _fox_kernel(q_ref, kt_ref, vt_ref, crow_ref, g_ref, o_ref, m_sc, l_sc, acc_sc, *, tb):
    qi = pl.program_id(2)
    q = q_ref[0]
    lane = lax.broadcasted_iota(jnp.int32, (tb, LANES), 1)
    lo = lane < FOX_HEAD_DIM
    qs = (jnp.where(lo, q, jnp.zeros_like(q)), jnp.where(lo, jnp.zeros_like(q), q))
    nc = tb // LANES
    m_sc[...] = jnp.full(m_sc.shape, NEG_INIT, F32)
    l_sc[...] = jnp.zeros(l_sc.shape, F32)
    acc_sc[...] = jnp.zeros(acc_sc.shape, F32)

    def step(kj, masked):
        start = pl.multiple_of(kj * tb, tb)
        kt = kt_ref[0, :, pl.ds(start, tb)]
        vt = vt_ref[0, :, pl.ds(start, tb)]
        parts, m_news, alphas, pbs = [], [], [], []
        for e in range(2):
            ck = crow_ref[0, 0, pl.ds(e, 1), pl.ds(start, tb)]
            s = _dot(qs[e], kt) - ck
            pe = [s[:, c * LANES:(c + 1) * LANES] for c in range(nc)]
            if masked:
                rows = lax.broadcasted_iota(jnp.int32, (tb, LANES), 0)
                cols = lax.broadcasted_iota(jnp.int32, (tb, LANES), 1)
                pe = [jnp.where(rows >= cols + c * LANES, p, MASK_VALUE) for c, p in enumerate(pe)]
            parts.append(pe)
        for e in range(2):
            mt = parts[e][0]
            for p in parts[e][1:]:
                mt = jnp.maximum(mt, p)
            m_prev = m_sc[e]
            m_new = jnp.maximum(m_prev, jnp.max(mt, axis=-1, keepdims=True))
            m_news.append(m_new)
            alphas.append(jnp.exp2(m_prev - m_new))
            m_sc[e] = m_new
        for e in range(2):
            ps = [jnp.exp2(p - m_news[e]) for p in parts[e]]
            lsum = ps[0]
            for p in ps[1:]:
                lsum = lsum + p
            l_sc[e] = alphas[e] * l_sc[e] + lsum
            pbs.append(jnp.concatenate([p.astype(BF16) for p in ps], axis=1))
        for e in range(2):
            acc_sc[e] = alphas[e] * acc_sc[e] + _dot_nt(pbs[e], vt)

    def body(kj, carry):
        step(kj, False)
        return carry

    lax.fori_loop(0, qi, body, 0)
    step(qi, True)
    o0 = acc_sc[0] / jnp.sum(l_sc[0], axis=-1, keepdims=True)
    o1 = acc_sc[1] / jnp.sum(l_sc[1], axis=-1, keepdims=True)
    o = jnp.where(lo, o0, o1)
    sq = o * o
    s_lo = jnp.sum(jnp.where(lo, sq, 0.0), axis=-1, keepdims=True)
    s_all = jnp.sum(sq, axis=-1, keepdims=True)
    ms = jnp.where(lo, s_lo, s_all - s_lo) * (1.0 / FOX_HEAD_DIM)
    o_ref[0] = (o * lax.rsqrt(ms + EPS) * g_ref[...]).astype(o_ref.dtype)


def _fox_prompt(q, ktb, vtb, crow, g):
    b, t, _ = q.shape
    tb = min(FOX_BLOCK, t)
    nhp = N_FOX_HEADS // 2
    crow4 = crow.reshape(b, nhp, 2, t)
    return pl.pallas_call(
        functools.partial(_fox_kernel, tb=tb),
        grid=(b, nhp, t // tb),
        in_specs=[pl.BlockSpec((1, tb, LANES), lambda i, h, j: (i, j, h)),
                  pl.BlockSpec((1, LANES, t), lambda i, h, j: (i, h, 0)),
                  pl.BlockSpec((1, LANES, t), lambda i, h, j: (i, h, 0)),
                  pl.BlockSpec((1, 1, 2, t), lambda i, h, j: (i, h, 0, 0)),
                  pl.BlockSpec((1, LANES), lambda i, h, j: (0, h))],
        out_specs=pl.BlockSpec((1, tb, LANES), lambda i, h, j: (i, j, h)),
        out_shape=jax.ShapeDtypeStruct((b, t, FOX_W), BF16),
        scratch_shapes=[pltpu.VMEM((2, tb, LANES), F32), pltpu.VMEM((2, tb, LANES), F32),
                        pltpu.VMEM((2, tb, LANES), F32)],
        compiler_params=_cparams("arbitrary", "arbitrary", "arbitrary"),
        name="fox",
    )(q, ktb, vtb, crow4, g)


def _decode_step(step, row_in, write_out, pt_ref, knew_ref, vnew_ref, g_ref, u_ref,
                 kc_ref, vc_ref, lfc_ref, kbuf, vbuf, lfbuf, sem, m_sc, l_sc, acc_sc, carry_sc,
                 *, gp, ng, nq, nb, layer):
    g = step % ng
    nr = nq * N_FOX_HEADS
    total = nb * ng
    ahead = DECODE_SLOTS - 1
    qbd, cn, bnew = row_in(step // ng)

    def page_copies(st, slot):
        row = st // ng
        grp = ng - 1 - st % ng
        out = []
        for i in range(gp):
            page = pt_ref[row, grp * gp + i]
            out.append(pltpu.make_async_copy(kc_ref.at[layer, page], kbuf.at[slot, i], sem.at[slot]))
            out.append(pltpu.make_async_copy(vc_ref.at[layer, page], vbuf.at[slot, i], sem.at[slot]))
            out.append(pltpu.make_async_copy(lfc_ref.at[layer, page], lfbuf.at[slot, i], sem.at[slot]))
        return out

    @pl.when(step == 0)
    def _():
        for st in range(min(ahead, total)):
            for c in page_copies(st, st):
                c.start()

    @pl.when(step + ahead < total)
    def _():
        for c in page_copies(step + ahead, (step + ahead) % DECODE_SLOTS):
            c.start()

    slot = step % DECODE_SLOTS
    for c in page_copies(step, slot):
        c.wait()
    k_refs = [kbuf.at[slot, i] for i in range(gp)]
    v_refs = [vbuf.at[slot, i] for i in range(gp)]
    lf_refs = [lfbuf.at[slot, i] for i in range(gp)]

    @pl.when(g == 0)
    def _():
        m_sc[...] = jnp.full(m_sc.shape, NEG_INIT, F32)
        l_sc[...] = jnp.zeros(l_sc.shape, F32)
        acc_sc[...] = jnp.zeros(acc_sc.shape, F32)
        carry_sc[...] = jnp.zeros(carry_sc.shape, F32)

    def update(s, vt):
        m_prev = m_sc[...]
        m_new = jnp.maximum(m_prev, jnp.max(s, axis=-1, keepdims=True))
        alpha = jnp.exp(m_prev - m_new)
        p = jnp.exp(s - m_new)
        l_sc[...] = alpha * l_sc[...] + jnp.sum(p, axis=-1, keepdims=True)
        acc_sc[...] = alpha * acc_sc[...] + _dot_nt(p.astype(BF16), vt)
        m_sc[...] = m_new

    lf_all = jnp.concatenate([r[...] for r in lf_refs], axis=0)
    r = _dot(jnp.concatenate(_split3(lf_all), axis=0), u_ref[...])
    r = r[:8 * gp] + r[8 * gp:16 * gp] + r[16 * gp:]
    carry = carry_sc[...]
    biases = [None] * gp
    for i in range(gp - 1, -1, -1):
        sl = slice(i * N_FOX_HEADS, (i + 1) * N_FOX_HEADS)
        biases[i] = jnp.concatenate([r[sl, :LANES] + carry] * nq, axis=0)
        carry = carry + r[sl, LANES:]
    carry_sc[...] = carry
    bias = jnp.concatenate(biases, axis=1) + cn
    kt = jnp.concatenate([r[...].reshape(FOX_W, LANES).astype(BF16) for r in k_refs], axis=1)
    vt = jnp.concatenate([r[...].reshape(FOX_W, LANES).astype(BF16) for r in v_refs], axis=1)
    update(_dot(qbd, kt) + bias, vt)

    @pl.when(g == ng - 1)
    def _():
        s_new = _dot(qbd, knew_ref[0].astype(BF16)) + bnew
        s_new = jnp.where(bnew > 0.5 * MASK_VALUE, s_new, MASK_VALUE)
        update(s_new, vnew_ref[0].astype(BF16))
        o = acc_sc[...] / l_sc[...]
        rowh = lax.broadcasted_iota(jnp.int32, (nr, FOX_W), 0) % N_FOX_HEADS
        colh = lax.broadcasted_iota(jnp.int32, (nr, FOX_W), 1) // FOX_HEAD_DIM
        o = jnp.where(rowh == colh, o, 0.0)
        ms = jnp.sum(o * o, axis=-1, keepdims=True) * (1.0 / FOX_HEAD_DIM)
        o = o * lax.rsqrt(ms + EPS) * g_ref[...]
        write_out(step // ng, jnp.sum(o.reshape(nq, N_FOX_HEADS, FOX_W), axis=1))


def _decode_scratch(gp, nr):
    page = (N_FOX_HEADS, FOX_HEAD_DIM, LANES)
    return [pltpu.VMEM((DECODE_SLOTS, gp) + page, F32),
            pltpu.VMEM((DECODE_SLOTS, gp) + page, F32),
            pltpu.VMEM((DECODE_SLOTS, gp, N_FOX_HEADS, LANES), F32),
            pltpu.SemaphoreType.DMA((DECODE_SLOTS,)),
            pltpu.VMEM((nr, 1), F32), pltpu.VMEM((nr, 1), F32),
            pltpu.VMEM((nr, FOX_W), F32), pltpu.VMEM((N_FOX_HEADS, LANES), F32)]


def _suffix_matrix():
    u = np.zeros((LANES, 2 * LANES), np.float32)
    u[:, :LANES] = np.tril(np.ones((LANES, LANES), np.float32), -1)
    u[:, LANES:] = 1.0
    return jnp.asarray(u, BF16)


def _decode_kernel(pt_ref, qbd_ref, cn_ref, knew_ref, vnew_ref, bnew_ref, g_ref, u_ref,
                   kc_ref, vc_ref, lfc_ref, o_ref, *scratch, ng, **statics):
    def write_out(row, val):
        o_ref[0] = val

    _decode_step(pl.program_id(0) * ng + pl.program_id(1),
                 lambda row: (qbd_ref[0], cn_ref[0], bnew_ref[0]), write_out,
                 pt_ref, knew_ref, vnew_ref, g_ref, u_ref, kc_ref, vc_ref, lfc_ref, *scratch,
                 ng=ng, **statics)


def _decode(page_table, qbd, cn, knew, vnew, bnew, g, kt_cache, vt_cache, lft_cache, layer):
    bs, n_pages = page_table.shape
    nr = qbd.shape[1]
    nq = nr // N_FOX_HEADS
    gp = min(PAGES_PER_STEP, n_pages)
    ng = n_pages // gp
    hbm = pl.BlockSpec(memory_space=pl.ANY)
    in_specs = [pl.BlockSpec((1, nr, FOX_W), lambda b, s, pt: (b, 0, 0)),
                pl.BlockSpec((1, nr, 1), lambda b, s, pt: (b, 0, 0)),
                pl.BlockSpec(knew.shape, lambda b, s, pt: (0, 0, 0)),
                pl.BlockSpec(vnew.shape, lambda b, s, pt: (0, 0, 0)),
                pl.BlockSpec((1, nr, bnew.shape[2]), lambda b, s, pt: (b, 0, 0)),
                pl.BlockSpec((1, FOX_W), lambda b, s, pt: (0, 0)),
                pl.BlockSpec((LANES, 2 * LANES), lambda b, s, pt: (0, 0)),
                hbm, hbm, hbm]
    grid_spec = pltpu.PrefetchScalarGridSpec(
        num_scalar_prefetch=1, grid=(bs, ng), in_specs=in_specs,
        out_specs=pl.BlockSpec((1, nq, FOX_W), lambda b, s, pt: (b, 0, 0)),
        scratch_shapes=_decode_scratch(gp, nr))
    return pl.pallas_call(
        functools.partial(_decode_kernel, gp=gp, ng=ng, nq=nq, nb=bs, layer=layer),
        grid_spec=grid_spec,
        out_shape=jax.ShapeDtypeStruct((bs, nq, FOX_W), F32),
        compiler_params=_cparams("arbitrary", "arbitrary"),
        name="decode",
    )(page_table, qbd, cn, knew, vnew, bnew, g, _suffix_matrix(), kt_cache, vt_cache, lft_cache)


def _hgrn_tables(L):
    nlev = int(np.log2(L))
    rows = []
    t = np.arange(L)[:, None]
    r = np.arange(L)[None, :]
    for n in range(nlev):
        h = L >> (n + 1)
        mid = (t // (2 * h)) * (2 * h) + h
        up = t >= mid
        rows.append(np.where(up, (r >= mid) & (r <= t), (r > t) & (r < mid)))
    rows.append(r <= t)
    rows.append(r > t)
    m_all = np.concatenate(rows, axis=0).astype(np.float32)
    masks = []
    for n in range(nlev):
        h = L >> (n + 1)
        masks.append((t // (2 * h)) == (r // (2 * h)))
    masks.append(t == r)
    return m_all, np.stack(masks).astype(np.float32), nlev


def _hgrn_kernel(*refs, L, nlev, t_valid, dec):
    if dec is None:
        (hq_ref, hf_ref, hi_ref, hg_ref, lb_ref, g_ref, s0_ref, mall_ref, mask_ref,
         o_ref, sout_ref, st_sc) = refs
    else:
        pt_ref = refs[0]
        (hq_ref, hf_ref, hi_ref, hg_ref, lb_ref, g_ref, s0_ref, mall_ref, mask_ref) = refs[1:10]
        (qbd_ref, cn_ref, knew_ref, vnew_ref, bnew_ref, fg_ref, u_ref,
         kc_ref, vc_ref, lfc_ref) = refs[10:20]
        o_ref, sout_ref, fo_ref, st_sc = refs[20:24]
        dec_scratch = refs[24:]
        dec = dict(dec)
        per_chunk = dec.pop("per_chunk")

        def write_out(row, val):
            fo_ref[row] = val
    nbat, tb = hq_ref.shape[0], hq_ref.shape[1]
    hd = HGRN_HEAD_DIM
    ti = pl.program_id(1)
    chains = [(bi, hh) for bi in range(nbat) for hh in range(N_HGRN_HEADS)]

    @pl.when(ti == 0)
    def _():
        for bi, hh in chains:
            st_sc[bi, hh] = s0_ref[bi, hh].T

    rowi = lax.broadcasted_iota(jnp.int32, (L, hd), 0)

    def chunk(c, carry):
        start = pl.multiple_of(c * L, L)
        rows = pl.ds(start, L)
        qs, kks, vvs, es, accs = [], [], [], [], []
        for bi, hh in chains:
            cols = slice(hh * hd, (hh + 1) * hd)
            lb = lb_ref[:, cols]
            q = hq_ref[bi, rows, cols]
            f = lb + (1.0 - lb) * _sigmoid(hf_ref[bi, rows, cols])
            lf = jnp.log(jnp.maximum(f, TINY))
            kk = 1.0 - f
            if t_valid is not None:
                valid = (rowi + start) < t_valid
                lf = jnp.where(valid, lf, 0.0)
                kk = jnp.where(valid, kk, 0.0)
            p0, p1, _ = _split3(lf)
            res = _dot(mall_ref[...], jnp.concatenate([p0, p1], axis=1))
            qs.append(q)
            kks.append(kk)
            vvs.append(_silu(hi_ref[bi, rows, cols]).astype(BF16))
            es.append(res)
        for i in range(len(chains)):
            es[i] = jnp.exp(es[i][:, :hd] + es[i][:, hd:])
            accs.append(mask_ref[nlev] * _dot_nt(qs[i].astype(BF16), kks[i].astype(BF16)))
        for n in range(nlev):
            up = (rowi & (L >> (n + 1))) != 0
            for i in range(len(chains)):
                en = es[i][n * L:(n + 1) * L]
                qd = jnp.where(up, qs[i] * en, 0.0).astype(BF16)
                kd = jnp.where(up, 0.0, kks[i] * en).astype(BF16)
                accs[i] = accs[i] + mask_ref[n] * _dot_nt(qd, kd)
        outs = []
        for i, (bi, hh) in enumerate(chains):
            eb = es[i][nlev * L:(nlev + 1) * L]
            er = es[i][(nlev + 1) * L:]
            st = st_sc[bi, hh]
            outs.append(_dot(accs[i].astype(BF16), vvs[i])
                        + _dot_nt((qs[i] * eb).astype(BF16), st.astype(BF16)))
            st_sc[bi, hh] = st * eb[L - 1:L, :] + _dot_tn(vvs[i], (kks[i] * er).astype(BF16))
        for i, (bi, hh) in enumerate(chains):
            cols = slice(hh * hd, (hh + 1) * hd)
            o = _rms(outs[i]) * g_ref[:, cols] * _silu(hg_ref[bi, rows, cols])
            o_ref[bi, rows, cols] = o.astype(o_ref.dtype)
        if dec is not None:
            it = (pl.program_id(0) * pl.num_programs(1) + ti) * (tb // L) + c
            for k in range(per_chunk):
                _decode_step(it * per_chunk + k,
                             lambda row: (qbd_ref[row], cn_ref[row], bnew_ref[row]), write_out,
                             pt_ref, knew_ref, vnew_ref, fg_ref, u_ref, kc_ref, vc_ref, lfc_ref,
                             *dec_scratch, **dec)
        return carry

    lax.fori_loop(0, tb // L, chunk, 0)

    @pl.when(ti == pl.num_programs(1) - 1)
    def _():
        for bi, hh in chains:
            sout_ref[bi, hh] = st_sc[bi, hh].T


def _hgrn_grid(b, t_pad):
    nbat = HGRN_ROWS_PER_STEP if b % HGRN_ROWS_PER_STEP == 0 else 1
    tb = min(HGRN_TIME_BLOCK, t_pad)
    return nbat, tb, (b // nbat) * (t_pad // tb) * (tb // HGRN_CHUNK)


def _hgrn(hq, hf, hi, hg, lb, g, s0, t_valid=None, decode_args=None):
    b, t_pad, _ = hq.shape
    L = HGRN_CHUNK
    nbat, tb, n_chunks = _hgrn_grid(b, t_pad)
    assert t_valid is None or tb == t_pad
    m_all, masks, nlev = _hgrn_tables(L)
    im = (lambda f: f) if decode_args is None else (lambda f: (lambda i, j, pt: f(i, j)))
    act = pl.BlockSpec((nbat, tb, HGRN_W), im(lambda i, j: (i, j, 0)))
    vec = pl.BlockSpec((1, HGRN_W), im(lambda i, j: (0, 0)))
    st = pl.BlockSpec((nbat, N_HGRN_HEADS, HGRN_HEAD_DIM, HGRN_HEAD_DIM),
                      im(lambda i, j: (i, 0, 0, 0)))
    const = lambda a: pl.BlockSpec(a.shape, im(lambda i, j: (0,) * a.ndim))
    mall_b, masks_j = jnp.asarray(m_all, BF16), jnp.asarray(masks)
    in_specs = [act, act, act, act, vec, vec, st, const(mall_b), const(masks_j)]
    args = [hq, hf, hi, hg, lb, g, s0, mall_b, masks_j]
    out_specs = [act, st]
    out_shape = [jax.ShapeDtypeStruct((b, t_pad, HGRN_W), BF16),
                 jax.ShapeDtypeStruct((b, N_HGRN_HEADS, HGRN_HEAD_DIM, HGRN_HEAD_DIM), F32)]
    scratch = [pltpu.VMEM((nbat, N_HGRN_HEADS, HGRN_HEAD_DIM, HGRN_HEAD_DIM), F32)]
    grid = (b // nbat, t_pad // tb)
    if decode_args is None:
        return pl.pallas_call(
            functools.partial(_hgrn_kernel, L=L, nlev=nlev, t_valid=t_valid, dec=None),
            grid=grid, in_specs=in_specs, out_specs=out_specs, out_shape=out_shape,
            scratch_shapes=scratch,
            compiler_params=_cparams("arbitrary", "arbitrary"),
            name="hgrn",
        )(*args)
    (page_table, qbd, cn, knew, vnew, bnew, fg, kt_cache, vt_cache, lft_cache, layer) = decode_args
    bs, n_pages = page_table.shape
    nr = qbd.shape[1]
    nq = nr // N_FOX_HEADS
    gp = min(PAGES_PER_STEP, n_pages)
    ng = n_pages // gp
    hbm = pl.BlockSpec(memory_space=pl.ANY)
    u = _suffix_matrix()
    in_specs += [const(qbd), const(cn), const(knew), const(vnew), const(bnew), const(fg), const(u),
                 hbm, hbm, hbm]
    args += [qbd, cn, knew, vnew, bnew, fg, u, kt_cache, vt_cache, lft_cache]
    out_specs.append(pl.BlockSpec((bs, nq, FOX_W), im(lambda i, j: (0, 0, 0))))
    out_shape.append(jax.ShapeDtypeStruct((bs, nq, FOX_W), F32))
    dec = dict(gp=gp, ng=ng, nq=nq, nb=bs, layer=layer, per_chunk=(bs * ng) // n_chunks)
    grid_spec = pltpu.PrefetchScalarGridSpec(
        num_scalar_prefetch=1, grid=grid, in_specs=in_specs, out_specs=out_specs,
        scratch_shapes=scratch + _decode_scratch(gp, nr))
    return pl.pallas_call(
        functools.partial(_hgrn_kernel, L=L, nlev=nlev, t_valid=t_valid, dec=dec),
        grid_spec=grid_spec, out_shape=out_shape,
        compiler_params=_cparams("arbitrary", "arbitrary"),
        name="hgrn_decode",
    )(page_table, *args)


def _decode_fits_hgrn(page_table, b, t):
    bs, n_pages = page_table.shape
    steps = bs * (n_pages // min(PAGES_PER_STEP, n_pages))
    n_chunks = _hgrn_grid(b, t)[2]
    return steps % n_chunks == 0


def _outproj_kernel(fo_ref, ho_ref, x_ref, gate_ref, sc_ref, sh_ref, g_ref, w_ref, x1_ref, h2_ref):
    mixed = _dot(fo_ref[0], w_ref[0, :FOX_W, :]) + _dot(ho_ref[0], w_ref[0, FOX_W:, :])
    x1 = x_ref[0] + gate_ref[0] * mixed
    x1_ref[0] = x1
    h2 = _rms(x1) * g_ref[...]
    h2_ref[0] = (h2 * (1.0 + sc_ref[0]) + sh_ref[0]).astype(BF16)


def _outproj(fo, ho, x, gate, sc, sh, g, w, tm, layer):
    bx, tx, d = x.shape
    per_row = gate.shape[1] != 1
    mod_spec = pl.BlockSpec((1, tm if per_row else 1, d),
                            (lambda b, i: (b, i, 0)) if per_row else (lambda b, i: (b, 0, 0)))
    row = lambda c: pl.BlockSpec((1, tm, c), lambda b, i: (b, i, 0))
    return pl.pallas_call(
        _outproj_kernel,
        grid=(bx, tx // tm),
        in_specs=[row(FOX_W), row(HGRN_W), row(d), mod_spec, mod_spec, mod_spec,
                  pl.BlockSpec((1, d), lambda b, i: (0, 0)),
                  pl.BlockSpec((1,) + w.shape[1:], lambda b, i: (layer, 0, 0))],
        out_specs=[row(d), row(d)],
        out_shape=[jax.ShapeDtypeStruct((bx, tx, d), F32), jax.ShapeDtypeStruct((bx, tx, d), BF16)],
        compiler_params=_cparams("arbitrary", "arbitrary"),
        name="outproj",
    )(fo, ho, x, gate, sc, sh, g, w)


def _ffn_kernel(*refs, tm, nj, period, final):
    if period is None:
        (h2_ref, x1_ref, gate_ref, w_ref, cw_ref, cb_ref, wd_ref, gf_ref,
         xo_ref, cs_ref, gated_sc, halo_sc) = refs
    else:
        (h2_ref, x1_ref, gate_ref, w_ref, cw_ref, cb_ref, wd_ref, gf_ref, p1_ref, p2_ref,
         xo_ref, cs_ref, gated_sc) = refs
    i = pl.program_id(1)
    h2 = h2_ref[0]
    tn = D_FF // nj
    rowi = lax.broadcasted_iota(jnp.int32, (tm, tn), 0)
    if period is None:
        @pl.when(i == 0)
        def _():
            halo_sc[...] = jnp.zeros(halo_sc.shape, F32)

    for j in range(nj):
        cols = slice(j * tn, (j + 1) * tn)
        a = _dot(h2, w_ref[0, :, cols])
        bb = _dot(h2, w_ref[0, :, D_FF + j * tn:D_FF + (j + 1) * tn])
        a1 = pltpu.roll(a, 1, axis=0)
        a2 = pltpu.roll(a, 2, axis=0)
        if period is None:
            hl = halo_sc[j]
            a1 = jnp.where(rowi == 0, hl[7:8, :], a1)
            a2 = jnp.where(rowi == 0, hl[6:7, :], jnp.where(rowi == 1, hl[7:8, :], a2))
            halo_sc[j] = a[tm - 8:, :]
            cs_ref[0, 0, :, cols] = a[tm - 2:, :]
        else:
            tpos = rowi % period
            a1 = jnp.where(tpos >= 1, a1, p1_ref[0, :, cols])
            a2 = jnp.where(tpos >= 2, a2, p2_ref[0, :, cols])
            cs_ref[0, :, cols] = a
        cw = cw_ref[:, cols]
        a_conv = cb_ref[:, cols] + cw[0:1, :] * a2 + cw[1:2, :] * a1 + cw[2:3, :] * a
        gated_sc[:, cols] = (_gelu_tanh(a_conv) * bb).astype(BF16)
    x2 = x1_ref[0] + gate_ref[0] * _dot(gated_sc[...], wd_ref[0])
    if final:
        x2 = _rms(x2) * gf_ref[...]
    xo_ref[0] = x2


def _ffn(h2, x1, gate, w_up, cw, cb, wd, gf, tm, final, layer, prev=None):
    bx, tx, d = x1.shape
    nj = D_FF // 256
    per_row = gate.shape[1] != 1
    mod_spec = pl.BlockSpec((1, tm if per_row else 1, d),
                            (lambda b, i: (b, i, 0)) if per_row else (lambda b, i: (b, 0, 0)))
    row = pl.BlockSpec((1, tm, d), lambda b, i: (b, i, 0))
    const = lambda a: pl.BlockSpec(a.shape, lambda b, i: (0,) * a.ndim, pipeline_mode=pl.Buffered(1))
    lay = lambda a: pl.BlockSpec((1,) + a.shape[1:], lambda b, i: (layer, 0, 0),
                                 pipeline_mode=pl.Buffered(1))
    in_specs = [row, row, mod_spec, lay(w_up), const(cw), const(cb), lay(wd), const(gf)]
    args = [h2, x1, gate, w_up, cw, cb, wd, gf]
    scratch = [pltpu.VMEM((tm, D_FF), BF16)]
    if prev is None:
        period = None
        cs_spec = pl.BlockSpec((1, 1, 2, D_FF), lambda b, i: (b, i, 0, 0))
        cs_shape = jax.ShapeDtypeStruct((bx, tx // tm, 2, D_FF), F32)
        scratch.append(pltpu.VMEM((nj, 8, D_FF // nj), F32))
    else:
        period = prev[2]
        in_specs += [pl.BlockSpec((1, tm, D_FF), lambda b, i: (b, i, 0))] * 2
        args += [prev[0], prev[1]]
        cs_spec = pl.BlockSpec((1, tm, D_FF), lambda b, i: (b, i, 0))
        cs_shape = jax.ShapeDtypeStruct((bx, tx, D_FF), F32)
    return pl.pallas_call(
        functools.partial(_ffn_kernel, tm=tm, nj=nj, period=period, final=final),
        grid=(bx, tx // tm),
        in_specs=in_specs,
        out_specs=[row, cs_spec],
        out_shape=[jax.ShapeDtypeStruct((bx, tx, d), F32), cs_shape],
        scratch_shapes=scratch,
        compiler_params=_cparams("arbitrary", "arbitrary"),
        name="ffn",
    )(*args)


def kernel(x_prompt, x_sample, c_prompt, c_sample, cache_k, cache_v, cache_logf, page_table, state_hgrn, state_conv, norm_g, final_norm_g, w_ada, b_ada, w_in, fox_fbias, fox_onorm_g, hgrn_lb_logits, hgrn_onorm_g, w_out, w_up, conv_w, conv_b, w_down):
    depth = w_in.shape[0]
    bp, tp, d = x_prompt.shape
    bs, ts, _ = x_sample.shape
    ns = bs * ts
    nh, hd = N_FOX_HEADS, FOX_HEAD_DIM

    p = jax.nn.softmax(hgrn_lb_logits.astype(F32), axis=0)
    lb_all = jnp.maximum(_running_sum(p, axis=0) - p[0:1], 0.0)

    kt_cache = cache_k.transpose(0, 1, 3, 4, 2)
    vt_cache = cache_v.transpose(0, 1, 3, 4, 2)
    lft_cache = cache_logf.transpose(0, 1, 3, 2)

    mod = _ada(jnp.concatenate([c_prompt, c_sample], axis=0), w_ada, b_ada)
    gf = final_norm_g.reshape(1, d)

    off = 3 * FOX_W + nh
    wn = jnp.concatenate([w_in[:, :, :FOX_W], w_in[:, :, off:]], axis=2).astype(BF16)
    wkv = w_in[:, :, FOX_W:3 * FOX_W].transpose(0, 2, 1).astype(BF16)
    wft = w_in[:, :, 3 * FOX_W:off].transpose(0, 2, 1).astype(BF16)
    wo = w_out.astype(BF16)
    wu = w_up.astype(BF16)
    wd = w_down.astype(BF16)

    xp = x_prompt
    xs = x_sample.reshape(1, ns, d)
    outs_p = [[] for _ in range(5)]
    outs_s = [[] for _ in range(5)]
    kv_p = kv_s = None
    for l in range(depth):
        mp = mod[l, :bp].reshape(bp, 6, 1, d)
        ms = jnp.repeat(mod[l, bp:].reshape(bs, 6, d), ts, axis=0).reshape(1, ns, 6, d)
        ms = jnp.moveaxis(ms, 2, 0)
        mods_p = [mp[:, j] for j in range(6)]
        mods_s = [ms[j] for j in range(6)]

        fbt = fox_fbias[l].reshape(nh, 1)
        g1 = norm_g[l, 0].reshape(1, d)
        g2 = norm_g[l, 1].reshape(1, d)
        fog = fox_onorm_g[l].reshape(1, FOX_W)
        hog = hgrn_onorm_g[l].reshape(1, HGRN_W)
        lb = lb_all[l].reshape(1, HGRN_W)
        cw = conv_w[l]
        cb = conv_b[l].reshape(1, D_FF)
        final = l == depth - 1

        (q_s, kt_s, vt_s, _, _, lft_s, hq_s, hf_s, hi_s, hg_s) = _inproj(
            xs, mods_s[1], mods_s[0], g1, wn, wkv, wft, fbt, tm=ns, layer=l,
            q_scale=FOX_SCALE, kv_prev=kv_s)
        kv_s = (kt_s, vt_s)
        lf3 = lft_s.reshape(nh, bs, ts).transpose(1, 2, 0)
        q4 = q_s.reshape(bs, ts, nh, hd)
        eye = jnp.eye(nh, dtype=BF16)
        qbd = (q4[:, :, :, None, :] * eye[None, None, :, :, None]).reshape(bs, ts * nh, FOX_W)
        cum = _running_sum(lf3, axis=1)
        cn = cum.reshape(bs, ts * nh, 1)
        col = jnp.arange(ns)
        ok = ((col[None, None, :] // ts == jnp.arange(bs)[:, None, None])
              & (col[None, None, :] % ts <= jnp.arange(ts)[None, :, None]))
        diff = cum[:, :, :, None] - cum.reshape(ns, nh).T[None, None, :, :]
        bnew = jnp.where(ok[:, :, None, :], diff, MASK_VALUE).reshape(bs, ts * nh, ns)
        decode_args = (page_table, qbd, cn, kt_s[l], vt_s[l], bnew, fog,
                       kt_cache, vt_cache, lft_cache, l)

        (q, kt, vt, ktb, vtb, lft, hq, hf, hi, hg) = _inproj(
            xp, mods_p[1], mods_p[0], g1, wn, wkv, wft, fbt, tm=512, layer=l,
            q_scale=FOX_SCALE * LOG2E, kv_prev=kv_p)
        kv_p = (kt, vt)
        fo = _fox_prompt(q, ktb, vtb, _cumsum(lft), fog)
        s0p = jnp.zeros((bp, N_HGRN_HEADS, HGRN_HEAD_DIM, HGRN_HEAD_DIM), F32)
        if _decode_fits_hgrn(page_table, bp, tp):
            ho, s_new, fo_s = _hgrn(hq, hf, hi, hg, lb, hog, s0p, decode_args=decode_args)
        else:
            ho, s_new = _hgrn(hq, hf, hi, hg, lb, hog, s0p)
            fo_s = _decode(*decode_args)
        x1, h2 = _outproj(fo, ho, xp, mods_p[2], mods_p[4], mods_p[3], g2, wo, tm=512, layer=l)
        xp, conv_p = _ffn(h2, x1, mods_p[5], wu, cw, cb, wd, gf, tm=512, final=final, layer=l)
        outs_p[2].append(lft.transpose(0, 2, 1))
        outs_p[3].append(s_new)
        outs_p[4].append(conv_p[:, -1])

        fo_s = fo_s.reshape(1, ns, FOX_W).astype(BF16)
        t_pad = HGRN_CHUNK
        padh = lambda a: jnp.pad(a.reshape(bs, ts, HGRN_W), ((0, 0), (0, t_pad - ts), (0, 0)))
        ho_s, s_new_s = _hgrn(padh(hq_s), padh(hf_s), padh(hi_s), padh(hg_s), lb, hog,
                              state_hgrn[l], t_valid=ts)
        ho_s = ho_s[:, :ts].reshape(1, ns, HGRN_W)
        x1, h2 = _outproj(fo_s, ho_s, xs, mods_s[2], mods_s[4], mods_s[3], g2, wo, tm=ns, layer=l)
        buf = state_conv[l]
        zrow = jnp.zeros((bs, 1, D_FF), F32)
        prev1 = jnp.concatenate([buf[:, 1:2], zrow, zrow, zrow], axis=1).reshape(1, ns, D_FF)
        prev2 = jnp.concatenate([buf[:, 0:1], buf[:, 1:2], zrow, zrow], axis=1).reshape(1, ns, D_FF)
        xs, a_full = _ffn(h2, x1, mods_s[5], wu, cw, cb, wd, gf, tm=ns, final=final, layer=l,
                          prev=(prev1, prev2, ts))
        outs_s[2].append(lf3)
        outs_s[3].append(s_new_s)
        outs_s[4].append(a_full.reshape(bs, ts, D_FF)[:, ts - 2:])

    kv_prompt = [a.reshape(depth, bp, nh, hd, tp).transpose(0, 1, 4, 2, 3) for a in kv_p]
    kv_sample = [a[:, 0].transpose(0, 2, 1).reshape(depth, bs, ts, nh, hd) for a in kv_s]
    return (xp, xs.reshape(bs, ts, d),
            *kv_prompt, *[jnp.stack(o) for o in outs_p[2:]],
            *kv_sample, *[jnp.stack(o) for o in outs_s[2:]])
```
